```python
import jax, jax.numpy as jnp
from jax import lax
import numpy as np

D_MODEL = 1024
BATCH = 16
SEQ = 4096
DEPTH = 1
DEC_BATCH = 2
DEC_SEQ = 16384
PAST_LEN = 128

N_META = 16
CHUNK = 64
CONV_K = 5
EPS = 1e-6
GDN_HEADS = 4
GDN_DK = 128
GDN_DV = 128
MLSTM_HEADS = 4
MLSTM_DQK = 64
MLSTM_DV = 128
MIX_WIDTH = GDN_HEADS * GDN_DV + MLSTM_HEADS * MLSTM_DV
D_FF = ((-(-8 * D_MODEL // 3) + 255) // 256) * 256
CONV_CH = 2 * GDN_HEADS * GDN_DK + GDN_HEADS * GDN_DV
SPLITS = (
    CONV_CH,
    GDN_HEADS * GDN_DV,
    2 * GDN_HEADS,
    2 * GDN_HEADS,
    MLSTM_HEADS * MLSTM_DQK,
    MLSTM_HEADS * MLSTM_DQK,
    MLSTM_HEADS * MLSTM_DV,
    MLSTM_HEADS * MLSTM_DV,
    2 * MLSTM_HEADS,
    2 * MLSTM_HEADS,
)
PROJ_WIDTH = sum(SPLITS)
SPLIT_IDX = [int(s) for s in np.cumsum(SPLITS)[:-1]]

kernel_name = "hybrid_gdn_mlstm_bidir_encoder"


def rmsnorm(x, w):
    xf = x.astype(jnp.float32)
    y = xf * lax.rsqrt(jnp.mean(xf * xf, axis=-1, keepdims=True) + EPS)
    return (y * w.astype(jnp.float32)).astype(x.dtype)


def l2norm(x):
    return x * lax.rsqrt(jnp.sum(x * x, axis=-1, keepdims=True) + EPS)


def centred_dwconv(x, w):
    pad = (CONV_K - 1) // 2
    return lax.conv_general_dilated(
        x, w[:, None, :].astype(x.dtype), window_strides=(1,), padding=[(pad, pad)],
        dimension_numbers=('NWC', 'WIO', 'NWC'), feature_group_count=x.shape[-1])


def _to_chunks(a):
    b, t = a.shape[0], a.shape[1]
    return jnp.moveaxis(a.reshape(b, t // CHUNK, CHUNK, *a.shape[2:]), 1, 0)


def _from_chunks(a):
    a = jnp.moveaxis(a, 0, 1)
    return a.reshape(a.shape[0], a.shape[1] * a.shape[2], *a.shape[3:])


def bidirectional_chunked(step, state0, fwd_inputs, bwd_inputs):
    n_tot = fwd_inputs[0].shape[1]
    t_real = n_tot - N_META
    meta = tuple(a[:, :N_META] for a in fwd_inputs)
    real = tuple(_to_chunks(a[:, N_META:]) for a in fwd_inputs)
    st, y_meta = step(state0, meta)
    _, y_real = lax.scan(step, st, real)
    y_f = jnp.concatenate([y_meta, _from_chunks(y_real)], axis=1)
    rev = [a[:, ::-1] for a in bwd_inputs]
    real_r = tuple(_to_chunks(a[:, :t_real]) for a in rev)
    meta_r = tuple(a[:, t_real:] for a in rev)
    st, y_real_r = lax.scan(step, state0, real_r)
    _, y_meta_r = step(st, meta_r)
    y_b = jnp.concatenate([_from_chunks(y_real_r), y_meta_r], axis=1)[:, ::-1]
    return y_f + y_b


def gdn_chunk_step(S, inputs):
    q, k, v, g, beta = inputs
    L = q.shape[1]
    dv = v.shape[-1]
    incl = jnp.tril(jnp.ones((L, L), dtype=bool))
    strict = jnp.tril(jnp.ones((L, L), dtype=bool), -1)
    g = jnp.cumsum(g, axis=1)
    gh = jnp.swapaxes(g, 1, 2)
    decay = jnp.exp(jnp.where(incl, gh[..., :, None] - gh[..., None, :], -jnp.inf))
    bh = jnp.swapaxes(beta, 1, 2)
    kk = jnp.einsum('bthd,bshd->bhts', k, k)
    a_low = jnp.where(strict, bh[..., :, None] * kk * decay, 0.0)
    rhs = jnp.concatenate([beta[..., None] * v, beta[..., None] * k * jnp.exp(g)[..., None]], axis=-1)
    rhs = jnp.swapaxes(rhs, 1, 2)
    uw = lax.linalg.triangular_solve(a_low + jnp.eye(L, dtype=a_low.dtype), rhs,
                                     left_side=True, lower=True)
    u, w = uw[..., :dv], uw[..., dv:]
    v_new = u - jnp.einsum('bhtk,bhkv->bhtv', w, S)
    qk = jnp.einsum('bthd,bshd->bhts', q, k) * decay
    o = (jnp.einsum('bthk,bhkv->bhtv', q * jnp.exp(g)[..., None], S)
         + jnp.einsum('bhts,bhsv->bhtv', qk, v_new))
    k_dec = k * jnp.exp(g[:, -1:, :] - g)[..., None]
    S = S * jnp.exp(gh[..., -1])[..., None, None] + jnp.einsum('blhk,bhlv->bhkv', k_dec, v_new)
    return S, jnp.swapaxes(o, 1, 2)


def mlstm_chunk_step(state, inputs):
    C, n, m = state
    q, k, v, ig, lf = inputs
    L = q.shape[1]
    incl = jnp.tril(jnp.ones((L, L), dtype=bool))
    b = jnp.swapaxes(jnp.cumsum(lf, axis=1), 1, 2)
    igh = jnp.swapaxes(ig, 1, 2)
    log_d = jnp.where(incl, b[..., :, None] - b[..., None, :] + igh[..., None, :], -jnp.inf)
    log_inter = b + m[..., None]
    m_t = jnp.maximum(jnp.max(log_d, axis=-1), log_inter)
    qk = jnp.einsum('bthd,bshd->bhts', q, k) * jnp.exp(log_d - m_t[..., None])
    w_inter = jnp.exp(log_inter - m_t)
    num = (jnp.einsum('bhts,bshv->bhtv', qk, v)
           + w_inter[..., None] * jnp.einsum('bthd,bhdv->bhtv', q, C))
    den = jnp.sum(qk, axis=-1) + w_inter * jnp.einsum('bthd,bhd->bht', q, n)
    h = num / jnp.maximum(jnp.abs(den), jnp.exp(-m_t))[..., None]
    b_last = b[..., -1]
    log_w = b_last[..., None] - b + igh
    m_new = jnp.maximum(b_last + m, jnp.max(log_w, axis=-1))
    w = jnp.exp(log_w - m_new[..., None])
    dec = jnp.exp(b_last + m - m_new)
    C = dec[..., None, None] * C + jnp.einsum('bhl,blhd,blhv->bhdv', w, k, v)
    n = dec[..., None] * n + jnp.einsum('bhl,blhd->bhd', w, k)
    return (C, n, m_new), jnp.swapaxes(h, 1, 2)


def gdn_mixer(qkv, z, a, b, conv_w, A_log, dt_bias, norm_w):
    bsz, n_tok = qkv.shape[0], qkv.shape[1]
    qkv = jax.nn.silu(centred_dwconv(qkv, conv_w.astype(jnp.float32)))
    q, k, v = jnp.split(qkv, [GDN_HEADS * GDN_DK, 2 * GDN_HEADS * GDN_DK], axis=-1)
    q = l2norm(q.reshape(bsz, n_tok, GDN_HEADS, GDN_DK)) * (GDN_DK ** -0.5)
    k = l2norm(k.reshape(bsz, n_tok, GDN_HEADS, GDN_DK))
    v = v.reshape(bsz, n_tok, GDN_HEADS, GDN_DV)
    g = -jnp.exp(A_log.astype(jnp.float32)) * jax.nn.softplus(a + dt_bias.astype(jnp.float32))
    beta = jax.nn.sigmoid(b)
    S0 = jnp.zeros((bsz, GDN_HEADS, GDN_DK, GDN_DV), jnp.float32)
    o = bidirectional_chunked(gdn_chunk_step, S0,
                              (q, k, v, g[:, :, 0], beta[:, :, 0]),
                              (q, k, v, g[:, :, 1], beta[:, :, 1]))
    o = rmsnorm(o, norm_w) * jax.nn.silu(z.reshape(bsz, n_tok, GDN_HEADS, GDN_DV))
    return o.reshape(bsz, n_tok, GDN_HEADS * GDN_DV)


def mlstm_mixer(q, k, v, o, i_pre, f_pre, i_bias, f_bias, norm_w):
    bsz, n_tok = q.shape[0], q.shape[1]
    q = q.reshape(bsz, n_tok, MLSTM_HEADS, MLSTM_DQK) * (MLSTM_DQK ** -0.5)
    k = k.reshape(bsz, n_tok, MLSTM_HEADS, MLSTM_DQK)
    v = v.reshape(bsz, n_tok, MLSTM_HEADS, MLSTM_DV)
    ig = i_pre + i_bias.astype(jnp.float32)
    lf = jax.nn.log_sigmoid(f_pre + f_bias.astype(jnp.float32))
    state0 = (jnp.zeros((bsz, MLSTM_HEADS, MLSTM_DQK, MLSTM_DV), jnp.float32),
              jnp.zeros((bsz, MLSTM_HEADS, MLSTM_DQK), jnp.float32),
              jnp.zeros((bsz, MLSTM_HEADS), jnp.float32))
    h = bidirectional_chunked(mlstm_chunk_step, state0,
                              (q, k, v, ig[:, :, 0], lf[:, :, 0]),
                              (q, k, v, ig[:, :, 1], lf[:, :, 1]))
    h = rmsnorm(h, norm_w.reshape(MLSTM_HEADS, MLSTM_DV)) * jax.nn.sigmoid(
        o.reshape(bsz, n_tok, MLSTM_HEADS, MLSTM_DV))
    return h.reshape(bsz, n_tok, MLSTM_HEADS * MLSTM_DV)


def encoder_layer(x, norm_pre_mix, w_in, conv_w, A_log, dt_bias, gdn_norm, i_bias, f_bias,
                  mlstm_norm, w_out, norm_post_mix, norm_pre_ffn, w_gate, w_up, w_down, norm_post_ffn):
    bsz, n_tok = x.shape[0], x.shape[1]
    h = rmsnorm(x, norm_pre_mix)
    proj = (h @ w_in).astype(jnp.float32)
    qkv_a, z_a, a_a, b_a, q_b, k_b, v_b, o_b, i_b, f_b = jnp.split(proj, SPLIT_IDX, axis=-1)
    gsh = (bsz, n_tok, 2, GDN_HEADS)
    msh = (bsz, n_tok, 2, MLSTM_HEADS)
    mix_a = gdn_mixer(qkv_a, z_a, a_a.reshape(gsh), b_a.reshape(gsh), conv_w, A_log, dt_bias, gdn_norm)
    mix_b = mlstm_mixer(q_b, k_b, v_b, o_b, i_b.reshape(msh), f_b.reshape(msh), i_bias, f_bias, mlstm_norm)
    mix = jnp.concatenate([mix_a, mix_b], axis=-1).astype(x.dtype) @ w_out
    x = x + rmsnorm(mix, norm_post_mix)
    h = rmsnorm(x, norm_pre_ffn)
    f = (jax.nn.silu(h @ w_gate) * (h @ w_up)) @ w_down
    return x + rmsnorm(f, norm_post_ffn)


def encode(x, meta_tokens, layer_params):
    bsz = x.shape[0]
    meta = jnp.broadcast_to(meta_tokens.astype(x.dtype)[None], (bsz, N_META, x.shape[-1]))
    h = jnp.concatenate([meta, x], axis=1)
    for l in range(DEPTH):
        h = encoder_layer(h, *[p[l] for p in layer_params])
    return h[:, N_META:]


def setup_inputs(seed: int = 0) -> dict:
    key = jax.random.key(seed)
    ks = jax.random.split(key, 20)
    f32 = jnp.float32

    def gain(k, n):
        return 1.0 + 0.02 * jax.random.normal(k, (DEPTH, n), f32)

    dt = jnp.exp(jax.random.uniform(ks[6], (DEPTH, 2, GDN_HEADS), f32)
                 * (float(np.log(0.1)) - float(np.log(0.001))) + float(np.log(0.001)))
    return {
        "x_prompt": jax.random.normal(ks[0], (BATCH, SEQ, D_MODEL), f32),
        "x_sample": jax.random.normal(ks[1], (DEC_BATCH, DEC_SEQ, D_MODEL), f32),
        "meta_tokens": jax.random.normal(ks[2], (N_META, D_MODEL), f32),
        "norm_pre_mix": gain(ks[3], D_MODEL),
        "w_in": jax.random.normal(ks[4], (DEPTH, D_MODEL, PROJ_WIDTH), f32) * D_MODEL ** -0.5,
        "conv_w": jax.random.normal(ks[5], (DEPTH, CONV_K, CONV_CH), f32) * CONV_K ** -0.5,
        "A_log": jnp.log(jax.random.uniform(ks[7], (DEPTH, 2, GDN_HEADS), f32, 1.0, 16.0)),
        "dt_bias": dt + jnp.log(-jnp.expm1(-dt)),
        "gdn_norm": gain(ks[8], GDN_DV),
        "i_bias": 0.5 * jax.random.normal(ks[9], (DEPTH, 2, MLSTM_HEADS), f32),
        "f_bias": jax.random.uniform(ks[10], (DEPTH, 2, MLSTM_HEADS), f32, 3.0, 6.0),
        "mlstm_norm": gain(ks[11], MLSTM_HEADS * MLSTM_DV),
        "w_out": jax.random.normal(ks[12], (DEPTH, MIX_WIDTH, D_MODEL), f32) * MIX_WIDTH ** -0.5,
        "norm_post_mix": gain(ks[13], D_MODEL),
        "norm_pre_ffn": gain(ks[14], D_MODEL),
        "w_gate": jax.random.normal(ks[15], (DEPTH, D_MODEL, D_FF), f32) * D_MODEL ** -0.5,
        "w_up": jax.random.normal(ks[16], (DEPTH, D_MODEL, D_FF), f32) * D_MODEL ** -0.5,
        "w_down": jax.random.normal(ks[17], (DEPTH, D_FF, D_MODEL), f32) * D_FF ** -0.5,
        "norm_post_ffn": gain(ks[18], D_MODEL),
    }


def reference(x_prompt, x_sample, meta_tokens, norm_pre_mix, w_in, conv_w, A_log, dt_bias, gdn_norm,
              i_bias, f_bias, mlstm_norm, w_out, norm_post_mix, norm_pre_ffn, w_gate, w_up, w_down,
              norm_post_ffn):
    layer_params = (norm_pre_mix, w_in, conv_w, A_log, dt_bias, gdn_norm, i_bias, f_bias, mlstm_norm,
                    w_out, norm_post_mix, norm_pre_ffn, w_gate, w_up, w_down, norm_post_ffn)
    y_prompt = encode(x_prompt, meta_tokens, layer_params)
    y_sample = encode(x_sample, meta_tokens, layer_params)
    return (y_prompt, y_sample)
```

```python
import functools

import jax
import jax.numpy as jnp
from jax import lax
from jax.experimental import pallas as pl
from jax.experimental.pallas import tpu as pltpu

D_MODEL = 1024
N_META = 16
CHUNK = 64
CONV_K = 5
EPS = 1e-6
GDN_HEADS = 4
GDN_DK = 128
GDN_DV = 128
MLSTM_HEADS = 4
MLSTM_DQK = 64
MLSTM_DV = 128
D_FF = 2816
N_PAIRS = 2
QKV_W = 3 * GDN_HEADS * GDN_DK
ZO_W = 2 * GDN_HEADS * GDN_DV
MQKV_W = 2 * MLSTM_HEADS * MLSTM_DQK + MLSTM_HEADS * MLSTM_DV
W1_COLS = QKV_W + ZO_W + MQKV_W
N_GATES = 4 * 2 * GDN_HEADS
LANES = 128
SUBLANES = 8
HALO = SUBLANES
CONV_ROWS = 128
META_ROWS = 2 * CHUNK
NEG_BIG = -1e30
VMEM_LIMIT = 56 * 1024 * 1024

R_GC, R_BETA, R_EG, R_EDB, R_R, R_CM, R_ET0, R_ET1, R_CML, R_BL, R_B = 0, 4, 8, 12, 16, 20, 24, 28, 32, 36, 44
RG_ROWS = 48

F32 = jnp.float32
BF16 = jnp.bfloat16


def _dot(a, b):
    return jnp.dot(a, b, preferred_element_type=F32)


def _dot_nt(a, b):
    return lax.dot_general(a, b, (((1,), (1,)), ((), ())), preferred_element_type=F32)


def _rms(v, w):
    ms = jnp.mean(v * v, axis=-1, keepdims=True)
    return v * lax.rsqrt(ms + EPS) * w


def _softplus(v):
    return jnp.maximum(v, 0.0) + jnp.log(1.0 + jnp.exp(-jnp.abs(v)))


def _lane_scan(cur, op, forward, lanemod):
    n = cur.shape[1]
    k = 1
    while k < CHUNK:
        if forward:
            shifted = pltpu.roll(cur, k, 1)
            ok = lanemod >= k
        else:
            shifted = pltpu.roll(cur, n - k, 1)
            ok = lanemod < CHUNK - k
        cur = jnp.where(ok, op(cur, shifted), cur)
        k *= 2
    return cur


def _pair_layout(a, b):
    tm = a.shape[1]
    low = (lax.broadcasted_iota(jnp.int32, (SUBLANES, tm), 1) & (LANES - 1)) < CHUNK
    even = jnp.where(low, a, pltpu.roll(b, CHUNK, 1))
    odd = jnp.where(low, pltpu.roll(a, tm - CHUNK, 1), b)
    cols = []
    for v in range(tm // LANES):
        cols += [even[:, v * LANES:(v + 1) * LANES], odd[:, v * LANES:(v + 1) * LANES]]
    return jnp.concatenate(cols, axis=1)


def _gate_tiles(hb, wg_ref, gadd_ref, galog_ref, tm, first_valid):
    gp = _dot_nt(wg_ref[...], hb)
    n = 2 * tm
    top = _pair_layout(gp[0:8], gp[16:24])
    bot = _pair_layout(gp[8:16], gp[24:32])
    row = lax.broadcasted_iota(jnp.int32, (SUBLANES, n), 0)
    lane = lax.broadcasted_iota(jnp.int32, (SUBLANES, n), 1)
    first4 = row < 4
    backward = (row & 2) != 0
    low = (lane & (LANES - 1)) < CHUNK
    lanemod = lane & (CHUNK - 1)
    reps = n // LANES
    gadd = gadd_ref[...]
    ytop = top + jnp.concatenate([gadd[0:8]] * reps, axis=1)
    ybot = bot + jnp.concatenate([gadd[8:16]] * reps, axis=1)
    neg_a = -jnp.exp(jnp.concatenate([galog_ref[0:8, :]] * reps, axis=1))
    etop = jnp.where(first4, neg_a * _softplus(ytop), jax.nn.sigmoid(ytop))
    ebot = jnp.where(first4, ybot, -_softplus(-ybot))
    if first_valid is not None:
        token = lax.shift_right_logical(lane, LANES.bit_length() - 1) * CHUNK + lanemod
        ok = token >= first_valid
        etop = jnp.where(ok, etop, 0.0)
        ebot = jnp.where(ok, ebot, jnp.where(first4, NEG_BIG, 0.0))
    add = lambda a, b: a + b
    swap = lambda a: pltpu.roll(a, 4, 0)

    pre = _lane_scan(etop, add, True, lanemod)
    suf = _lane_scan(etop, add, False, lanemod)
    gc = jnp.where(backward, suf, pre)
    tot = pre + suf - etop
    eg = jnp.exp(gc)
    edb = jnp.exp(tot - gc) * swap(etop)
    etot = jnp.exp(tot)
    et0 = jnp.where(low, etot, pltpu.roll(etot, CHUNK, 1))
    et1 = jnp.where(low, pltpu.roll(etot, n - CHUNK, 1), etot)
    y1 = jnp.where(first4, gc, etop)
    y2 = jnp.where(first4, eg, swap(edb))
    y4 = jnp.where(first4, et0, swap(et1))

    pre = _lane_scan(ebot, add, True, lanemod)
    suf = _lane_scan(ebot, add, False, lanemod)
    bsum = jnp.where(backward, suf, pre)
    btot = pre + suf - ebot
    r = ebot - swap(bsum)
    pmax = _lane_scan(r, jnp.maximum, True, lanemod)
    smax = _lane_scan(r, jnp.maximum, False, lanemod)
    cm = jnp.where(backward, smax, pmax)
    y3 = jnp.where(first4, r, swap(cm))
    y5 = jnp.where(first4, jnp.maximum(pmax, smax), btot)
    return [y1, y2, y3, y4, y5, bsum]


def _prep_rows(hext, hb, w1_ref, wg_ref, cw_ref, gadd_ref, galog_ref, gq_out, tm, first_valid):
    ext = _dot(hext, w1_ref[:, 0:QKV_W])
    nrow = tm + 2 * HALO
    taps = [pltpu.roll(ext, ((CONV_K - 1) // 2 - j) % nrow, 0) if j != (CONV_K - 1) // 2 else ext
            for j in range(CONV_K)]

    for r0 in range(0, tm, CONV_ROWS):
        acc = None
        for j in range(CONV_K):
            term = cw_ref[j:j + 1, :] * taps[j][HALO + r0:HALO + r0 + CONV_ROWS]
            acc = term if acc is None else acc + term
        qkv = acc * jax.nn.sigmoid(acc)
        if first_valid is not None:
            rows = r0 + lax.broadcasted_iota(jnp.int32, (CONV_ROWS, 1), 0)
            qkv = jnp.where(rows >= first_valid, qkv, 0.0)
        parts = []
        for idx in range(2 * GDN_HEADS):
            xh = qkv[:, idx * GDN_DK:(idx + 1) * GDN_DK]
            scale = lax.rsqrt(jnp.sum(xh * xh, axis=-1, keepdims=True) + EPS)
            if idx < GDN_HEADS:
                scale = scale * (GDN_DK ** -0.5)
            parts.append((xh * scale).astype(BF16))
        parts.append(qkv[:, 2 * GDN_HEADS * GDN_DK:].astype(BF16))
        gq_out[pl.ds(r0, CONV_ROWS), :] = jnp.concatenate(parts, axis=1)

    pz = _dot(hb, w1_ref[:, QKV_W:QKV_W + ZO_W // 2])
    po = _dot(hb, w1_ref[:, QKV_W + ZO_W // 2:QKV_W + ZO_W])
    zo = jnp.concatenate([(pz * jax.nn.sigmoid(pz)).astype(BF16),
                          jax.nn.sigmoid(po).astype(BF16)], axis=1)

    c0 = QKV_W + ZO_W
    pm = _dot(hb, w1_ref[:, c0:c0 + MQKV_W])
    nq = MLSTM_HEADS * MLSTM_DQK
    mqkv = jnp.concatenate([(pm[:, 0:nq] * (MLSTM_DQK ** -0.5)).astype(BF16),
                            pm[:, nq:].astype(BF16)], axis=1)
    return zo, mqkv, _gate_tiles(hb, wg_ref, gadd_ref, galog_ref, tm, first_valid)


class _Chain:
    pass


def _bd(x, half):
    lane = lax.broadcasted_iota(jnp.int32, x.shape, 1)
    zero = jnp.zeros_like(x)
    return jnp.concatenate([jnp.where(lane < half, x, zero), jnp.where(lane >= half, x, zero)], axis=0)


def _rows_to_cols(rows):
    pad = jnp.zeros((SUBLANES - len(rows), LANES), F32)
    return jnp.concatenate(list(rows) + [pad], axis=0).T


def _pair_cols(cols, j, low):
    first = jnp.broadcast_to(cols[:CHUNK, j:j + 1], (CHUNK, LANES))
    second = jnp.broadcast_to(cols[CHUNK:, j:j + 1], (CHUNK, LANES))
    return jnp.where(low, first, second)


def _chunk_steps(inputs, dirs, states, want_out):
    ti = lax.broadcasted_iota(jnp.int32, (CHUNK, LANES), 0)
    si = lax.broadcasted_iota(jnp.int32, (CHUNK, LANES), 1) & (CHUNK - 1)
    eye2 = jnp.where(ti == si, 1.0, 0.0)
    low = lax.broadcasted_iota(jnp.int32, (CHUNK, LANES), 1) < CHUNK
    low_row = lax.broadcasted_iota(jnp.int32, (1, LANES), 1) < CHUNK
    ones_v = jnp.ones((CHUNK, MLSTM_DV), BF16)
    nq = MLSTM_HEADS * MLSTM_DQK

    gd, ml = [], []
    steps = []
    m_final = []
    for d, chunks, (_, _, m_in) in zip(dirs, inputs, states):
        incl = ti >= si if d == 0 else ti <= si
        strict = ti > si if d == 0 else ti < si
        m_run = list(m_in)
        per_chunk = []
        for gq, mq, rg in chunks:
            g_step, m_step = [], []
            for p in range(N_PAIRS):
                row = lambda base: rg[base + 2 * d + p:base + 2 * d + p + 1, :]
                c = _Chain()
                c.incl, c.strict = incl, strict
                c.q = gq[:, 2 * GDN_DK * p:2 * GDN_DK * (p + 1)]
                c.k = gq[:, GDN_HEADS * GDN_DK + 2 * GDN_DK * p:GDN_HEADS * GDN_DK + 2 * GDN_DK * (p + 1)]
                c.v = gq[:, 2 * GDN_HEADS * GDN_DK + 2 * GDN_DV * p:2 * GDN_HEADS * GDN_DK + 2 * GDN_DV * (p + 1)]
                c.gc, c.beta, c.eg, c.edb = row(R_GC), row(R_BETA), row(R_EG), row(R_EDB)
                c.etot = jnp.concatenate([row(R_ET0), row(R_ET1)], axis=1)
                g_step.append(c)
                c = _Chain()
                c.incl = incl
                c.q = mq[:, 2 * MLSTM_DQK * p:2 * MLSTM_DQK * (p + 1)]
                c.k = mq[:, nq + 2 * MLSTM_DQK * p:nq + 2 * MLSTM_DQK * (p + 1)]
                v = mq[:, 2 * nq + 2 * MLSTM_DV * p:2 * nq + 2 * MLSTM_DV * (p + 1)]
                c.va = jnp.concatenate([v[:, :MLSTM_DV], ones_v, v[:, MLSTM_DV:], ones_v], axis=1)
                c.m = m_run[p]
                c.r, c.b = row(R_R), row(R_B)
                c.mt = jnp.maximum(c.m, row(R_CM))
                c.ml = jnp.maximum(c.m, row(R_CML))
                m_run[p] = row(R_BL) + c.ml
                m_step.append(c)
            per_chunk.append((g_step, m_step))
            gd += g_step
            ml += m_step
        steps.append(per_chunk)
        m_final.append(m_run)

    def stage_qk(gd, ml):
        for c, cm in zip(gd, ml):
            cols = _rows_to_cols([c.gc, cm.mt, -(cm.b + cm.mt)])
            c.gccol = _pair_cols(cols, 0, low)
            cm.mtcol = _pair_cols(cols, 1, low)
            if want_out:
                cm.emt = [jnp.exp(jnp.broadcast_to(cols[:CHUNK, 2:3], (CHUNK, LANES))),
                          jnp.exp(jnp.broadcast_to(cols[CHUNK:, 2:3], (CHUNK, LANES)))]
        for c in gd:
            c.kmask = _bd(c.k, GDN_DK)
            kst = jnp.concatenate([c.k[:, :GDN_DK], c.k[:, GDN_DK:]], axis=0)
            c.kt = kst.T
            kq = _dot(jnp.concatenate([c.q, c.k], axis=0), _bd(c.kt, CHUNK))
            c.qk, c.kk = kq[:CHUNK], kq[CHUNK:]
        for c in ml:
            c.kt = c.k.T
            kt2 = jnp.concatenate([c.kt, c.kt], axis=1)
            rowh = lax.broadcasted_iota(jnp.int32, (2 * MLSTM_DQK, LANES), 0) < MLSTM_DQK
            laneh = lax.broadcasted_iota(jnp.int32, (2 * MLSTM_DQK, LANES), 1) < CHUNK
            c.qk = _dot(c.q, jnp.where(rowh == laneh, kt2, jnp.zeros_like(kt2)))

    def stage_a(gd, ml):
        for c in gd:
            decay = jnp.where(c.incl, jnp.exp(jnp.where(c.incl, c.gccol - c.gc, 0.0)), 0.0)
            a = jnp.where(c.strict, c.kk * decay, 0.0) * c.beta
            c.aqkb = c.qk * decay * c.beta
            c.p = eye2 - a
            ab = a.astype(BF16)
            c.b = _dot(ab, _bd(ab, CHUNK))
        for c in ml:
            w_row = jnp.exp(c.r - c.ml)
            dec = jnp.exp(c.m - c.ml)
            d0 = jnp.where(low_row, dec, pltpu.roll(dec, CHUNK, 1))
            d1 = jnp.where(low_row, pltpu.roll(dec, CHUNK, 1), dec)
            w1 = pltpu.roll(w_row, CHUNK, 1)
            wmat = jnp.concatenate([jnp.broadcast_to(w_row[:, :CHUNK], (MLSTM_DQK, CHUNK)),
                                    jnp.broadcast_to(w1[:, :CHUNK], (MLSTM_DQK, CHUNK))], axis=0)
            ktw = (c.kt.astype(F32) * wmat).astype(BF16)
            c.upd = jnp.concatenate([_dot(ktw[:MLSTM_DQK], c.va[:, :2 * MLSTM_DV]),
                                     _dot(ktw[MLSTM_DQK:], c.va[:, 2 * MLSTM_DV:])], axis=1)
            c.dec = jnp.concatenate([d0, d0, d1, d1], axis=1)

    def stage_power(gd, ml):
        for c in gd:
            bb = c.b.astype(BF16)
            x = _dot(jnp.concatenate([c.p.astype(BF16), bb], axis=0), _bd(bb, CHUNK))
            c.p = c.p + x[:CHUNK]
            c.b = x[CHUNK:]

    def stage_inverse(gd, ml):
        if want_out:
            for c in ml:
                dm = jnp.where(c.incl, jnp.exp(jnp.where(c.incl, c.r - c.mtcol, 0.0)), 0.0)
                c.lhs_t = jnp.concatenate([c.qk * dm, eye2 * jnp.exp(c.m - c.mt)], axis=1).astype(BF16)
                c.va_bd = _bd(c.va, 2 * MLSTM_DV)
        for c in gd:
            c.t = c.p + _dot(c.p.astype(BF16), _bd(c.b.astype(BF16), CHUNK))

    def stage_uw(gd, ml):
        for c in gd:
            c.ut = _dot(c.t.astype(BF16), _bd(c.v, GDN_DV))
            wt = _dot((c.t * c.eg).astype(BF16), c.kmask)
            c.lhs_s = jnp.concatenate([wt.astype(BF16), c.q], axis=0)
            c.kdt = (c.kt.astype(F32) * c.edb).astype(BF16)
            if want_out:
                c.lhs_o = jnp.concatenate([c.aqkb, eye2 * c.eg], axis=1).astype(BF16)

    n_levels = 0
    n = 2
    while 2 * n < CHUNK:
        n_levels += 1
        n *= 2
    stages = [stage_qk, stage_a] + [stage_power] * n_levels + [stage_inverse, stage_uw]

    s_run = [list(st[0]) for st in states]
    c_run = [list(st[1]) for st in states]
    outs = [[] for _ in dirs]

    def state_step(ci):
        now = [(i, p, steps[i][ci][0][p], steps[i][ci][1][p]) for i in range(len(dirs)) for p in range(N_PAIRS)]
        for i, p, g, m in now:
            g.wq = _dot(g.lhs_s, _bd(s_run[i][p].astype(BF16), GDN_DV))
        if want_out:
            for i, p, g, m in now:
                m.qc = _dot(m.q, _bd(c_run[i][p].astype(BF16), 2 * MLSTM_DV))
        for i, p, g, m in now:
            vbd = _bd((g.ut - g.wq[:CHUNK]).astype(BF16), GDN_DV)
            s_run[i][p] = s_run[i][p] * g.etot + _dot(g.kdt, vbd)
            if want_out:
                rhs = jnp.concatenate([vbd, _bd(g.wq[CHUNK:].astype(BF16), GDN_DV)], axis=0)
                g.o = _dot(g.lhs_o, rhs)
        for i, p, g, m in now:
            if want_out:
                rhs = jnp.concatenate([m.va_bd, _bd(m.qc.astype(BF16), 2 * MLSTM_DV)], axis=0)
                m.tot = _dot(m.lhs_t, rhs)
            c_run[i][p] = c_run[i][p] * m.dec + m.upd
        if want_out:
            for i in range(len(dirs)):
                pieces = [g.o for _, _, g, _ in now[i * N_PAIRS:(i + 1) * N_PAIRS]]
                for _, _, _, m in now[i * N_PAIRS:(i + 1) * N_PAIRS]:
                    for h in range(2):
                        num = m.tot[:, 2 * MLSTM_DV * h:2 * MLSTM_DV * h + MLSTM_DV]
                        den = m.tot[:, 2 * MLSTM_DV * h + MLSTM_DV:2 * MLSTM_DV * (h + 1)]
                        pieces.append(num / jnp.maximum(jnp.abs(den), m.emt[h]))
                outs[i].append(pieces)

    n_chunks = len(inputs[0])
    for step in range(n_chunks + len(stages)):
        for ci in range(n_chunks):
            s = step - ci
            if 0 <= s < len(stages):
                stages[s]([g for st in steps for g in st[ci][0]], [m for st in steps for m in st[ci][1]])
            elif s == len(stages):
                state_step(ci)
    return [(s_run[i], c_run[i], m_final[i]) for i in range(len(dirs))], outs


def _load_state(s_ref, c_ref, m_ref):
    return ([s_ref[p] for p in range(N_PAIRS)], [c_ref[p] for p in range(N_PAIRS)],
            [m_ref[p][0:1, :] for p in range(N_PAIRS)])


def _store_state(state, s_ref, c_ref, m_ref):
    s_out, c_out, m_out = state
    for p in range(N_PAIRS):
        s_ref[p] = s_out[p]
        c_ref[p] = c_out[p]
        m_ref[p] = jnp.broadcast_to(m_out[p], (SUBLANES, LANES))


def _prep_kernel(x_ref, xl_ref, xr_ref, meta_ref, nw_ref, w1_ref, wg_ref, cw_ref, gadd_ref, galog_ref,
                 gqkv_ref, zo_ref, mqkv_ref, rg_ref, *, tm):
    i = pl.program_id(1)
    last = pl.num_programs(1) - 1
    nw = nw_ref[...]
    left = jnp.where(i == 0, meta_ref[...], xl_ref[0])
    right = xr_ref[0] * (i < last).astype(F32)
    hmain = _rms(x_ref[0], nw)
    hext = jnp.concatenate([_rms(left, nw), hmain, _rms(right, nw)], axis=0).astype(BF16)
    zo, mqkv, tiles = _prep_rows(hext, hmain.astype(BF16), w1_ref, wg_ref, cw_ref, gadd_ref, galog_ref,
                                 gqkv_ref.at[0], tm, None)
    zo_ref[0] = zo
    mqkv_ref[0] = mqkv
    for c in range(tm // CHUNK):
        for k, y in enumerate(tiles):
            rg_ref[0, c, k * SUBLANES:(k + 1) * SUBLANES, :] = y[:, c * LANES:(c + 1) * LANES]


def _meta_kernel(xh_ref, meta_ref, nw_ref, w1_ref, wg_ref, cw_ref, gadd_ref, galog_ref,
                 s0_ref, c0_ref, m0_ref, gq_ref):
    nw = nw_ref[...]
    npad = META_ROWS - N_META
    hmain = jnp.concatenate([jnp.zeros((npad, D_MODEL), F32), _rms(meta_ref[...], nw)], axis=0)
    hext = jnp.concatenate([jnp.zeros((HALO, D_MODEL), F32), hmain, _rms(xh_ref[0], nw)], axis=0).astype(BF16)
    _, mqkv, tiles = _prep_rows(hext, hmain.astype(BF16), w1_ref, wg_ref, cw_ref, gadd_ref, galog_ref,
                                gq_ref, META_ROWS, npad)
    last = META_ROWS // CHUNK - 1
    rg = jnp.concatenate([y[:, last * LANES:(last + 1) * LANES] for y in tiles], axis=0)
    state = ([jnp.zeros((GDN_DK, 2 * GDN_DV), F32)] * N_PAIRS,
             [jnp.zeros((MLSTM_DQK, 4 * MLSTM_DV), F32)] * N_PAIRS,
             [jnp.zeros((1, LANES), F32)] * N_PAIRS)
    chunk = (gq_ref[pl.ds(last * CHUNK, CHUNK), :], mqkv[last * CHUNK:], rg)
    (new_state,), _ = _chunk_steps([[chunk]], (0,), [state], False)
    _store_state(new_state, s0_ref.at[0], c0_ref.at[0], m0_ref.at[0])


def _scan_kernel(gqf_ref, mqf_ref, rgf_ref, gqb_ref, mqb_ref, rgb_ref, s0_ref, c0_ref, m0_ref,
                 of_ref, ob_ref, s_ref, c_ref, m_ref, *, cb):
    j = pl.program_id(1)

    @pl.when(j == 0)
    def _():
        s_ref[0] = s0_ref[0]
        c_ref[0] = c0_ref[0]
        m_ref[0] = m0_ref[0]
        s_ref[1] = jnp.zeros(s_ref.shape[1:], F32)
        c_ref[1] = jnp.zeros(c_ref.shape[1:], F32)
        m_ref[1] = jnp.zeros(m_ref.shape[1:], F32)

    chunk = lambda gq, mq, rg, c: (gq[0, c * CHUNK:(c + 1) * CHUNK, :], mq[0, c * CHUNK:(c + 1) * CHUNK, :], rg[0, c])
    order_b = list(range(cb - 1, -1, -1))
    in_f = [chunk(gqf_ref, mqf_ref, rgf_ref, c) for c in range(cb)]
    in_b = [chunk(gqb_ref, mqb_ref, rgb_ref, c) for c in order_b]
    st_f = (s_ref.at[0], c_ref.at[0], m_ref.at[0])
    st_b = (s_ref.at[1], c_ref.at[1], m_ref.at[1])
    (new_f, new_b), (out_f, out_b) = _chunk_steps([in_f, in_b], (0, 1), [_load_state(*st_f), _load_state(*st_b)], True)
    _store_state(new_f, *st_f)
    _store_state(new_b, *st_b)
    for c in range(cb):
        of_ref[0, c * CHUNK:(c + 1) * CHUNK, :] = jnp.concatenate(out_f[c], axis=1).astype(of_ref.dtype)
        cr = order_b[c]
        ob_ref[0, cr * CHUNK:(cr + 1) * CHUNK, :] = jnp.concatenate(out_b[c], axis=1).astype(ob_ref.dtype)


def _ffn_kernel(x_ref, of_ref, ob_ref, zo_ref, hn_ref, wout_ref, n1_ref, n2_ref, wg_ref, wu_ref, wd_ref, n3_ref,
                y_ref, *, ff_chunk):
    x = x_ref[...]
    o = of_ref[...].astype(F32) + ob_ref[...].astype(F32)
    zo = zo_ref[...].astype(F32)
    hn = hn_ref[...]
    parts = []
    for h in range(GDN_HEADS + MLSTM_HEADS):
        sl = slice(h * LANES, (h + 1) * LANES)
        parts.append((_rms(o[:, sl], hn[:, sl]) * zo[:, sl]).astype(BF16))
    mix = _dot(jnp.concatenate(parts, axis=1), wout_ref[...])
    x1 = x + _rms(mix, n1_ref[...])
    h2 = _rms(x1, n2_ref[...]).astype(BF16)
    f = None
    for c in range(D_FF // ff_chunk):
        sl = slice(c * ff_chunk, (c + 1) * ff_chunk)
        gate = _dot(h2, wg_ref[:, sl])
        up = _dot(h2, wu_ref[:, sl])
        act = (gate * jax.nn.sigmoid(gate) * up).astype(BF16)
        part = _dot(act, wd_ref[sl, :])
        f = part if f is None else f + part
    y_ref[...] = x1 + _rms(f, n3_ref[...])


def _const_spec(shape):
    nd = len(shape)
    return pl.BlockSpec(shape, lambda *_: (0,) * nd, pipeline_mode=pl.Buffered(1))


def _encode(x, p):
    bsz, t, _ = x.shape
    tm = min(512, t)
    nt = t // tm
    cb = min(8, t // CHUNK)
    tb = cb * CHUNK
    nj = t // tb
    nc = t // CHUNK
    cparams = functools.partial(pltpu.CompilerParams, vmem_limit_bytes=VMEM_LIMIT)
    weight_specs = [
        _const_spec((1, D_MODEL)),
        _const_spec((D_MODEL, W1_COLS)),
        _const_spec((N_GATES, D_MODEL)),
        _const_spec((SUBLANES, QKV_W)),
        _const_spec((2 * SUBLANES, LANES)),
        _const_spec((2 * SUBLANES, LANES)),
    ]
    weights = (p["norm_pre_mix"], p["w1"], p["wg"], p["conv_w"], p["gadd"], p["galog"])
    s_shape = (N_PAIRS, GDN_DK, 2 * GDN_DV)
    c_shape = (N_PAIRS, MLSTM_DQK, 4 * MLSTM_DV)
    m_shape = (N_PAIRS, SUBLANES, LANES)
    state_specs = [pl.BlockSpec((1,) + s, lambda b, *_: (b, 0, 0, 0)) for s in (s_shape, c_shape, m_shape)]

    s0, c0, m0 = pl.pallas_call(
        _meta_kernel,
        grid=(bsz,),
        in_specs=[pl.BlockSpec((1, HALO, D_MODEL), lambda b: (b, 0, 0)),
                  _const_spec((N_META, D_MODEL))] + weight_specs,
        out_specs=state_specs,
        out_shape=[jax.ShapeDtypeStruct((bsz,) + s, F32) for s in (s_shape, c_shape, m_shape)],
        scratch_shapes=[
            pltpu.VMEM((META_ROWS, QKV_W), BF16),
        ],
        compiler_params=cparams(dimension_semantics=("arbitrary",)),
        name="meta_state",
    )(x, p["meta"], *weights)

    tpb = tm // HALO
    gqkv, zo, mqkv, rg = pl.pallas_call(
        functools.partial(_prep_kernel, tm=tm),
        grid=(bsz, nt),
        in_specs=[
            pl.BlockSpec((1, tm, D_MODEL), lambda b, i: (b, i, 0)),
            pl.BlockSpec((1, HALO, D_MODEL), lambda b, i: (b, jnp.maximum(i * tpb - 1, 0), 0)),
            pl.BlockSpec((1, HALO, D_MODEL), lambda b, i: (b, jnp.minimum((i + 1) * tpb, t // HALO - 1), 0)),
            pl.BlockSpec((HALO, D_MODEL), lambda b, i: (N_META // HALO - 1, 0)),
        ] + weight_specs,
        out_specs=[
            pl.BlockSpec((1, tm, QKV_W), lambda b, i: (b, i, 0)),
            pl.BlockSpec((1, tm, ZO_W), lambda b, i: (b, i, 0)),
            pl.BlockSpec((1, tm, MQKV_W), lambda b, i: (b, i, 0)),
            pl.BlockSpec((1, tm // CHUNK, RG_ROWS, LANES), lambda b, i: (b, i, 0, 0)),
        ],
        out_shape=[
            jax.ShapeDtypeStruct((bsz, t, QKV_W), BF16),
            jax.ShapeDtypeStruct((bsz, t, ZO_W), BF16),
            jax.ShapeDtypeStruct((bsz, t, MQKV_W), BF16),
            jax.ShapeDtypeStruct((bsz, nc, RG_ROWS, LANES), F32),
        ],
        compiler_params=cparams(dimension_semantics=("arbitrary", "arbitrary")),
        name="token_prep",
    )(x, x, x, p["meta"], *weights)

    fwd = lambda b, j: (b, j, 0)
    bwd = lambda b, j: (b, nj - 1 - j, 0)
    fwd4 = lambda b, j: (b, j, 0, 0)
    bwd4 = lambda b, j: (b, nj - 1 - j, 0, 0)
    o_f, o_b = pl.pallas_call(
        functools.partial(_scan_kernel, cb=cb),
        grid=(bsz, nj),
        in_specs=[
            pl.BlockSpec((1, tb, QKV_W), fwd),
            pl.BlockSpec((1, tb, MQKV_W), fwd),
            pl.BlockSpec((1, cb, RG_ROWS, LANES), fwd4),
            pl.BlockSpec((1, tb, QKV_W), bwd),
            pl.BlockSpec((1, tb, MQKV_W), bwd),
            pl.BlockSpec((1, cb, RG_ROWS, LANES), bwd4),
        ] + state_specs,
        out_specs=[
            pl.BlockSpec((1, tb, ZO_W), fwd),
            pl.BlockSpec((1, tb, ZO_W), bwd),
        ],
        out_shape=[
            jax.ShapeDtypeStruct((bsz, t, ZO_W), BF16),
            jax.ShapeDtypeStruct((bsz, t, ZO_W), BF16),
        ],
        scratch_shapes=[
            pltpu.VMEM((2,) + s_shape, F32),
            pltpu.VMEM((2,) + c_shape, F32),
            pltpu.VMEM((2,) + m_shape, F32),
        ],
        compiler_params=cparams(dimension_semantics=("arbitrary", "arbitrary")),
        name="chunk_scan",
    )(gqkv, mqkv, rg, gqkv, mqkv, rg, s0, c0, m0)

    rows = bsz * t
    tr = min(512, rows)
    row_spec = lambda w: pl.BlockSpec((tr, w), lambda i: (i, 0))
    y = pl.pallas_call(
        functools.partial(_ffn_kernel, ff_chunk=D_FF // 2),
        grid=(rows // tr,),
        in_specs=[
            row_spec(D_MODEL), row_spec(ZO_W), row_spec(ZO_W), row_spec(ZO_W),
            _const_spec((1, ZO_W)),
            _const_spec((ZO_W, D_MODEL)),
            _const_spec((1, D_MODEL)),
            _const_spec((1, D_MODEL)),
            _const_spec((D_MODEL, D_FF)),
            _const_spec((D_MODEL, D_FF)),
            _const_spec((D_FF, D_MODEL)),
            _const_spec((1, D_MODEL)),
        ],
        out_specs=row_spec(D_MODEL),
        out_shape=jax.ShapeDtypeStruct((rows, D_MODEL), F32),
        compiler_params=cparams(dimension_semantics=("arbitrary",)),
        name="mix_ffn",
    )(x.reshape(rows, D_MODEL), o_f.reshape(rows, ZO_W), o_b.reshape(rows, ZO_W), zo.reshape(rows, ZO_W),
      p["head_norm"], p["w_out"], p["norm_post_mix"], p["norm_pre_ffn"],
      p["w_gate"], p["w_up"], p["w_down"], p["norm_post_ffn"])
    return y.reshape(bsz, t, D_MODEL)


def _pair_rows(v):
    return jnp.repeat(v.reshape(2 * N_PAIRS, 2), CHUNK, axis=1)


def _prepare_params(meta_tokens, norm_pre_mix, w_in, conv_w, A_log, dt_bias, gdn_norm, i_bias, f_bias,
                    mlstm_norm, w_out, norm_post_mix, norm_pre_ffn, w_gate, w_up, w_down, norm_post_ffn):
    w = w_in[0]
    o_qkv = 0
    o_z = o_qkv + QKV_W
    o_a = o_z + GDN_HEADS * GDN_DV
    o_b = o_a + 2 * GDN_HEADS
    o_mq = o_b + 2 * GDN_HEADS
    o_o = o_mq + MQKV_W
    o_i = o_o + MLSTM_HEADS * MLSTM_DV
    o_f = o_i + 2 * MLSTM_HEADS
    w1 = jnp.concatenate([w[:, o_qkv:o_z], w[:, o_z:o_a], w[:, o_o:o_i], w[:, o_mq:o_o]], axis=1).astype(BF16)
    gcols = jnp.stack([w[:, o:o + 2 * GDN_HEADS] for o in (o_a, o_b, o_i, o_f)], axis=1)
    gcols = gcols.reshape(D_MODEL, 4, 2, N_PAIRS, 2)
    wg = jnp.transpose(gcols, (4, 1, 2, 3, 0)).reshape(N_GATES, D_MODEL).astype(BF16)
    zeros4 = jnp.zeros((2 * N_PAIRS, LANES), F32)
    gadd = jnp.concatenate([_pair_rows(dt_bias[0]), zeros4, _pair_rows(i_bias[0]), _pair_rows(f_bias[0])], axis=0)
    galog = jnp.concatenate([_pair_rows(A_log[0]), zeros4, zeros4, zeros4], axis=0)
    cw = jnp.zeros((SUBLANES, QKV_W), F32).at[0:CONV_K].set(conv_w[0])
    head_norm = jnp.concatenate([jnp.tile(gdn_norm[0], GDN_HEADS), mlstm_norm[0]]).reshape(1, ZO_W)
    return {
        "meta": meta_tokens,
        "norm_pre_mix": norm_pre_mix[0].reshape(1, D_MODEL),
        "w1": w1,
        "wg": wg,
        "conv_w": cw,
        "gadd": gadd,
        "galog": galog,
        "head_norm": head_norm,
        "w_out": w_out[0].astype(BF16),
        "norm_post_mix": norm_post_mix[0].reshape(1, D_MODEL),
        "norm_pre_ffn": norm_pre_ffn[0].reshape(1, D_MODEL),
        "w_gate": w_gate[0].astype(BF16),
        "w_up": w_up[0].astype(BF16),
        "w_down": w_down[0].astype(BF16),
        "norm_post_ffn": norm_post_ffn[0].reshape(1, D_MODEL),
    }


def kernel(x_prompt, x_sample, meta_tokens, norm_pre_mix, w_in, conv_w, A_log, dt_bias, gdn_norm, i_bias, f_bias,
           mlstm_norm, w_out, norm_post_mix, norm_pre_ffn, w_gate, w_up, w_down, norm_post_ffn):
    p = _prepare_params(meta_tokens, norm_pre_mix, w_in, conv_w, A_log, dt_bias, gdn_norm, i_bias, f_bias,
                        mlstm_norm, w_out, norm_post_mix, norm_pre_ffn, w_gate, w_up, w_down, norm_post_ffn)
    return (_encode(x_prompt, p), _encode(x_sample, p))
```

```python
import functools

import jax
import jax.numpy as jnp
from jax import lax
from jax.experimental import pallas as pl
from jax.experimental.pallas import tpu as pltpu

D_MODEL = 1024
N_META = 16
CHUNK = 64
CONV_K = 5
EPS = 1e-6
GDN_HEADS = 4
GDN_DK = 128
GDN_DV = 128
MLSTM_HEADS = 4
MLSTM_DQK = 64
MLSTM_DV = 128
D_FF = 2816
N_PAIRS = 2
QKV_W = 3 * GDN_HEADS * GDN_DK
ZO_W = 2 * GDN_HEADS * GDN_DV
MQKV_W = 2 * MLSTM_HEADS * MLSTM_DQK + MLSTM_HEADS * MLSTM_DV
W1_COLS = QKV_W + ZO_W + MQKV_W
N_GATES = 4 * 2 * GDN_HEADS
LANES = 128
SUBLANES = 8
HALO = SUBLANES
CONV_ROWS = 128
META_ROWS = 2 * CHUNK
FFN_SUBTILES = 2
PREP_SUBTILES = 2
MXU_WIDTH = 256
FF_CHUNK = 6 * MXU_WIDTH
NEG_BIG = -1e30
VMEM_LIMIT = 56 * 1024 * 1024

R_GC, R_BETA, R_EG, R_EDB, R_R, R_CM, R_ET0, R_ET1, R_CML, R_BL, R_B = 0, 4, 8, 12, 16, 20, 24, 28, 32, 36, 44
RG_ROWS = 48

F32 = jnp.float32
BF16 = jnp.bfloat16


def _dot(a, b):
    return jnp.dot(a, b, preferred_element_type=F32)


def _dot_nt(a, b):
    return lax.dot_general(a, b, (((1,), (1,)), ((), ())), preferred_element_type=F32)


def _rms(v, w):
    ms = jnp.mean(v * v, axis=-1, keepdims=True)
    return v * lax.rsqrt(ms + EPS) * w


def _softplus(v):
    return jnp.maximum(v, 0.0) + jnp.log(1.0 + jnp.exp(-jnp.abs(v)))


def _lane_scan(cur, op, forward, lanemod):
    n = cur.shape[1]
    k = 1
    while k < CHUNK:
        if forward:
            shifted = pltpu.roll(cur, k, 1)
            ok = lanemod >= k
        else:
            shifted = pltpu.roll(cur, n - k, 1)
            ok = lanemod < CHUNK - k
        cur = jnp.where(ok, op(cur, shifted), cur)
        k *= 2
    return cur


def _pair_layout(a, b):
    tm = a.shape[1]
    low = (lax.broadcasted_iota(jnp.int32, (SUBLANES, tm), 1) & (LANES - 1)) < CHUNK
    even = jnp.where(low, a, pltpu.roll(b, CHUNK, 1))
    odd = jnp.where(low, pltpu.roll(a, tm - CHUNK, 1), b)
    cols = []
    for v in range(tm // LANES):
        cols += [even[:, v * LANES:(v + 1) * LANES], odd[:, v * LANES:(v + 1) * LANES]]
    return jnp.concatenate(cols, axis=1)


def _gate_tiles(gp, gadd_ref, galog_ref, tm, first_valid):
    n = 2 * tm
    top = _pair_layout(gp[0:8], gp[16:24])
    bot = _pair_layout(gp[8:16], gp[24:32])
    row = lax.broadcasted_iota(jnp.int32, (SUBLANES, n), 0)
    lane = lax.broadcasted_iota(jnp.int32, (SUBLANES, n), 1)
    first4 = row < 4
    backward = (row & 2) != 0
    low = (lane & (LANES - 1)) < CHUNK
    lanemod = lane & (CHUNK - 1)
    reps = n // LANES
    gadd = gadd_ref[...]
    ytop = top + jnp.concatenate([gadd[0:8]] * reps, axis=1)
    ybot = bot + jnp.concatenate([gadd[8:16]] * reps, axis=1)
    neg_a = -jnp.exp(jnp.concatenate([galog_ref[0:8, :]] * reps, axis=1))
    etop = jnp.where(first4, neg_a * _softplus(ytop), jax.nn.sigmoid(ytop))
    ebot = jnp.where(first4, ybot, -_softplus(-ybot))
    if first_valid is not None:
        token = lax.shift_right_logical(lane, LANES.bit_length() - 1) * CHUNK + lanemod
        ok = token >= first_valid
        etop = jnp.where(ok, etop, 0.0)
        ebot = jnp.where(ok, ebot, jnp.where(first4, NEG_BIG, 0.0))
    add = lambda a, b: a + b
    swap = lambda a: pltpu.roll(a, 4, 0)

    pre = _lane_scan(etop, add, True, lanemod)
    suf = _lane_scan(etop, add, False, lanemod)
    gc = jnp.where(backward, suf, pre)
    tot = pre + suf - etop
    eg = jnp.exp(gc)
    edb = jnp.exp(tot - gc) * swap(etop)
    etot = jnp.exp(tot)
    et0 = jnp.where(low, etot, pltpu.roll(etot, CHUNK, 1))
    et1 = jnp.where(low, pltpu.roll(etot, n - CHUNK, 1), etot)
    y1 = jnp.where(first4, gc, etop)
    y2 = jnp.where(first4, eg, swap(edb))
    y4 = jnp.where(first4, et0, swap(et1))

    pre = _lane_scan(ebot, add, True, lanemod)
    suf = _lane_scan(ebot, add, False, lanemod)
    bsum = jnp.where(backward, suf, pre)
    btot = pre + suf - ebot
    r = ebot - swap(bsum)
    pmax = _lane_scan(r, jnp.maximum, True, lanemod)
    smax = _lane_scan(r, jnp.maximum, False, lanemod)
    cm = jnp.where(backward, smax, pmax)
    y3 = jnp.where(first4, r, swap(cm))
    y5 = jnp.where(first4, jnp.maximum(pmax, smax), btot)
    return [y1, y2, y3, y4, y5, bsum]


def _prep_rows(hext, hb, w1_ref, wg_ref, cw_ref, gadd_ref, galog_ref, gq_out, tm, first_valid):
    gp = _dot_nt(wg_ref[...], hb)
    ext = _dot(hext, w1_ref[:, 0:QKV_W])
    c0 = QKV_W + ZO_W
    pz = _dot(hb, w1_ref[:, QKV_W:QKV_W + ZO_W // 2])
    po = _dot(hb, w1_ref[:, QKV_W + ZO_W // 2:QKV_W + ZO_W])
    pm = _dot(hb, w1_ref[:, c0:c0 + MQKV_W])
    yield
    tiles = _gate_tiles(gp, gadd_ref, galog_ref, tm, first_valid)
    nrow = tm + 2 * HALO
    taps = [pltpu.roll(ext, ((CONV_K - 1) // 2 - j) % nrow, 0) if j != (CONV_K - 1) // 2 else ext
            for j in range(CONV_K)]

    for r0 in range(0, tm, CONV_ROWS):
        acc = None
        for j in range(CONV_K):
            term = cw_ref[j:j + 1, :] * taps[j][HALO + r0:HALO + r0 + CONV_ROWS]
            acc = term if acc is None else acc + term
        qkv = acc * jax.nn.sigmoid(acc)
        if first_valid is not None:
            rows = r0 + lax.broadcasted_iota(jnp.int32, (CONV_ROWS, 1), 0)
            qkv = jnp.where(rows >= first_valid, qkv, 0.0)
        parts = []
        for idx in range(2 * GDN_HEADS):
            xh = qkv[:, idx * GDN_DK:(idx + 1) * GDN_DK]
            scale = lax.rsqrt(jnp.sum(xh * xh, axis=-1, keepdims=True) + EPS)
            if idx < GDN_HEADS:
                scale = scale * (GDN_DK ** -0.5)
            parts.append((xh * scale).astype(BF16))
        parts.append(qkv[:, 2 * GDN_HEADS * GDN_DK:].astype(BF16))
        gq_out[pl.ds(r0, CONV_ROWS), :] = jnp.concatenate(parts, axis=1)

    yield
    zo = jnp.concatenate([(pz * jax.nn.sigmoid(pz)).astype(BF16),
                          jax.nn.sigmoid(po).astype(BF16)], axis=1)
    nq = MLSTM_HEADS * MLSTM_DQK
    mqkv = jnp.concatenate([(pm[:, 0:nq] * (MLSTM_DQK ** -0.5)).astype(BF16),
                            pm[:, nq:].astype(BF16)], axis=1)
    return zo, mqkv, tiles


def _drive(gens):
    results = [None] * len(gens)
    live = list(range(len(gens)))
    while live:
        for i in list(live):
            try:
                next(gens[i])
            except StopIteration as stop:
                results[i] = stop.value
                live.remove(i)
    return results


class _Chain:
    pass


def _bd(x, half):
    lane = lax.broadcasted_iota(jnp.int32, x.shape, 1)
    zero = jnp.zeros_like(x)
    return jnp.concatenate([jnp.where(lane < half, x, zero), jnp.where(lane >= half, x, zero)], axis=0)


def _rows_to_cols(rows):
    pad = jnp.zeros((SUBLANES - len(rows), LANES), F32)
    return jnp.concatenate(list(rows) + [pad], axis=0).T


def _pair_cols(cols, j, low):
    first = jnp.broadcast_to(cols[:CHUNK, j:j + 1], (CHUNK, LANES))
    second = jnp.broadcast_to(cols[CHUNK:, j:j + 1], (CHUNK, LANES))
    return jnp.where(low, first, second)


def _chunk_steps(inputs, dirs, states, want_out):
    ti = lax.broadcasted_iota(jnp.int32, (CHUNK, LANES), 0)
    si = lax.broadcasted_iota(jnp.int32, (CHUNK, LANES), 1) & (CHUNK - 1)
    eye2 = jnp.where(ti == si, 1.0, 0.0)
    low = lax.broadcasted_iota(jnp.int32, (CHUNK, LANES), 1) < CHUNK
    low_row = lax.broadcasted_iota(jnp.int32, (1, LANES), 1) < CHUNK
    ones_v = jnp.ones((CHUNK, MLSTM_DV), BF16)
    nq = MLSTM_HEADS * MLSTM_DQK

    gd, ml = [], []
    steps = []
    m_final = []
    for d, chunks, (_, _, m_in) in zip(dirs, inputs, states):
        incl = ti >= si if d == 0 else ti <= si
        strict = ti > si if d == 0 else ti < si
        m_run = list(m_in)
        per_chunk = []
        for gq, mq, rg in chunks:
            g_step, m_step = [], []
            for p in range(N_PAIRS):
                row = lambda base: rg[base + 2 * d + p:base + 2 * d + p + 1, :]
                c = _Chain()
                c.incl, c.strict = incl, strict
                c.q = gq[:, 2 * GDN_DK * p:2 * GDN_DK * (p + 1)]
                c.k = gq[:, GDN_HEADS * GDN_DK + 2 * GDN_DK * p:GDN_HEADS * GDN_DK + 2 * GDN_DK * (p + 1)]
                c.v = gq[:, 2 * GDN_HEADS * GDN_DK + 2 * GDN_DV * p:2 * GDN_HEADS * GDN_DK + 2 * GDN_DV * (p + 1)]
                c.gc, c.beta, c.eg, c.edb = row(R_GC), row(R_BETA), row(R_EG), row(R_EDB)
                c.etot = jnp.concatenate([row(R_ET0), row(R_ET1)], axis=1)
                g_step.append(c)
                c = _Chain()
                c.incl = incl
                c.q = mq[:, 2 * MLSTM_DQK * p:2 * MLSTM_DQK * (p + 1)]
                c.k = mq[:, nq + 2 * MLSTM_DQK * p:nq + 2 * MLSTM_DQK * (p + 1)]
                v = mq[:, 2 * nq + 2 * MLSTM_DV * p:2 * nq + 2 * MLSTM_DV * (p + 1)]
                c.va = jnp.concatenate([v[:, :MLSTM_DV], ones_v, v[:, MLSTM_DV:], ones_v], axis=1)
                c.m = m_run[p]
                c.r, c.b = row(R_R), row(R_B)
                c.mt = jnp.maximum(c.m, row(R_CM))
                c.ml = jnp.maximum(c.m, row(R_CML))
                m_run[p] = row(R_BL) + c.ml
                m_step.append(c)
            per_chunk.append((g_step, m_step))
            gd += g_step
            ml += m_step
        steps.append(per_chunk)
        m_final.append(m_run)

    def stage_qk(gd, ml):
        for c, cm in zip(gd, ml):
            cols = _rows_to_cols([c.gc, cm.mt, -(cm.b + cm.mt)])
            c.gccol = _pair_cols(cols, 0, low)
            cm.mtcol = _pair_cols(cols, 1, low)
            if want_out:
                cm.emt = [jnp.exp(jnp.broadcast_to(cols[:CHUNK, 2:3], (CHUNK, LANES))),
                          jnp.exp(jnp.broadcast_to(cols[CHUNK:, 2:3], (CHUNK, LANES)))]
        for c in gd:
            c.kmask = _bd(c.k, GDN_DK)
            kst = jnp.concatenate([c.k[:, :GDN_DK], c.k[:, GDN_DK:]], axis=0)
            c.kt = kst.T
            kq = _dot(jnp.concatenate([c.q, c.k], axis=0), _bd(c.kt, CHUNK))
            c.qk, c.kk = kq[:CHUNK], kq[CHUNK:]
        for c in ml:
            c.kt = c.k.T
            kt2 = jnp.concatenate([c.kt, c.kt], axis=1)
            rowh = lax.broadcasted_iota(jnp.int32, (2 * MLSTM_DQK, LANES), 0) < MLSTM_DQK
            laneh = lax.broadcasted_iota(jnp.int32, (2 * MLSTM_DQK, LANES), 1) < CHUNK
            c.qk = _dot(c.q, jnp.where(rowh == laneh, kt2, jnp.zeros_like(kt2)))

    def stage_a(gd, ml):
        for c in gd:
            decay = jnp.where(c.incl, jnp.exp(jnp.where(c.incl, c.gccol - c.gc, 0.0)), 0.0)
            a = jnp.where(c.strict, c.kk * decay, 0.0) * c.beta
            c.aqkb = c.qk * decay * c.beta
            c.p = eye2 - a
            ab = a.astype(BF16)
            c.b = _dot(ab, _bd(ab, CHUNK))
        for c in ml:
            w_row = jnp.exp(c.r - c.ml)
            dec = jnp.exp(c.m - c.ml)
            d0 = jnp.where(low_row, dec, pltpu.roll(dec, CHUNK, 1))
            d1 = jnp.where(low_row, pltpu.roll(dec, CHUNK, 1), dec)
            w1 = pltpu.roll(w_row, CHUNK, 1)
            wmat = jnp.concatenate([jnp.broadcast_to(w_row[:, :CHUNK], (MLSTM_DQK, CHUNK)),
                                    jnp.broadcast_to(w1[:, :CHUNK], (MLSTM_DQK, CHUNK))], axis=0)
            ktw = (c.kt.astype(F32) * wmat).astype(BF16)
            c.upd = jnp.concatenate([_dot(ktw[:MLSTM_DQK], c.va[:, :2 * MLSTM_DV]),
                                     _dot(ktw[MLSTM_DQK:], c.va[:, 2 * MLSTM_DV:])], axis=1)
            c.dec = jnp.concatenate([d0, d0, d1, d1], axis=1)

    def stage_power(gd, ml):
        for c in gd:
            bb = c.b.astype(BF16)
            x = _dot(jnp.concatenate([c.p.astype(BF16), bb], axis=0), _bd(bb, CHUNK))
            c.p = c.p + x[:CHUNK]
            c.b = x[CHUNK:]

    def stage_inverse(gd, ml):
        if want_out:
            for c in ml:
                dm = jnp.where(c.incl, jnp.exp(jnp.where(c.incl, c.r - c.mtcol, 0.0)), 0.0)
                c.lhs_t = jnp.concatenate([c.qk * dm, eye2 * jnp.exp(c.m - c.mt)], axis=1).astype(BF16)
                c.va_bd = _bd(c.va, 2 * MLSTM_DV)
        for c in gd:
            c.t = c.p + _dot(c.p.astype(BF16), _bd(c.b.astype(BF16), CHUNK))

    def stage_uw(gd, ml):
        for c in gd:
            c.ut = _dot(c.t.astype(BF16), _bd(c.v, GDN_DV))
            wt = _dot((c.t * c.eg).astype(BF16), c.kmask)
            c.lhs_s = jnp.concatenate([wt.astype(BF16), c.q], axis=0)
            c.kdt = (c.kt.astype(F32) * c.edb).astype(BF16)
            if want_out:
                c.lhs_o = jnp.concatenate([c.aqkb, eye2 * c.eg], axis=1).astype(BF16)

    n_levels = 0
    n = 2
    while 2 * n < CHUNK:
        n_levels += 1
        n *= 2
    stages = [stage_qk, stage_a] + [stage_power] * n_levels + [stage_inverse, stage_uw]

    s_run = [list(st[0]) for st in states]
    c_run = [list(st[1]) for st in states]
    outs = [[] for _ in dirs]

    def state_step(ci):
        now = [(i, p, steps[i][ci][0][p], steps[i][ci][1][p]) for i in range(len(dirs)) for p in range(N_PAIRS)]
        for i, p, g, m in now:
            g.wq = _dot(g.lhs_s, _bd(s_run[i][p].astype(BF16), GDN_DV))
        if want_out:
            for i, p, g, m in now:
                m.qc = _dot(m.q, _bd(c_run[i][p].astype(BF16), 2 * MLSTM_DV))
        for i, p, g, m in now:
            vbd = _bd((g.ut - g.wq[:CHUNK]).astype(BF16), GDN_DV)
            s_run[i][p] = s_run[i][p] * g.etot + _dot(g.kdt, vbd)
            if want_out:
                rhs = jnp.concatenate([vbd, _bd(g.wq[CHUNK:].astype(BF16), GDN_DV)], axis=0)
                g.o = _dot(g.lhs_o, rhs)
        for i, p, g, m in now:
            if want_out:
                rhs = jnp.concatenate([m.va_bd, _bd(m.qc.astype(BF16), 2 * MLSTM_DV)], axis=0)
                m.tot = _dot(m.lhs_t, rhs)
            c_run[i][p] = c_run[i][p] * m.dec + m.upd
        if want_out:
            for i in range(len(dirs)):
                pieces = [g.o for _, _, g, _ in now[i * N_PAIRS:(i + 1) * N_PAIRS]]
                for _, _, _, m in now[i * N_PAIRS:(i + 1) * N_PAIRS]:
                    for h in range(2):
                        num = m.tot[:, 2 * MLSTM_DV * h:2 * MLSTM_DV * h + MLSTM_DV]
                        den = m.tot[:, 2 * MLSTM_DV * h + MLSTM_DV:2 * MLSTM_DV * (h + 1)]
                        pieces.append(num / jnp.maximum(jnp.abs(den), m.emt[h]))
                outs[i].append(pieces)

    n_chunks = len(inputs[0])
    for step in range(n_chunks + len(stages)):
        for ci in range(n_chunks):
            s = step - ci
            if 0 <= s < len(stages):
                stages[s]([g for st in steps for g in st[ci][0]], [m for st in steps for m in st[ci][1]])
            elif s == len(stages):
                state_step(ci)
    return [(s_run[i], c_run[i], m_final[i]) for i in range(len(dirs))], outs


def _load_state(s_ref, c_ref, m_ref):
    return ([s_ref[p] for p in range(N_PAIRS)], [c_ref[p] for p in range(N_PAIRS)],
            [m_ref[p][0:1, :] for p in range(N_PAIRS)])


def _store_state(state, s_ref, c_ref, m_ref):
    s_out, c_out, m_out = state
    for p in range(N_PAIRS):
        s_ref[p] = s_out[p]
        c_ref[p] = c_out[p]
        m_ref[p] = jnp.broadcast_to(m_out[p], (SUBLANES, LANES))


def _prep_kernel(x_ref, xl_ref, xr_ref, meta_ref, nw_ref, w1_ref, wg_ref, cw_ref, gadd_ref, galog_ref,
                 gqkv_ref, zo_ref, mqkv_ref, rg_ref, *, tm):
    i = pl.program_id(1)
    last = pl.num_programs(1) - 1
    nw = nw_ref[...]
    left = jnp.where(i == 0, meta_ref[...], xl_ref[0])
    right = xr_ref[0] * (i < last).astype(F32)
    hmain = _rms(x_ref[0], nw)
    hext = jnp.concatenate([_rms(left, nw), hmain, _rms(right, nw)], axis=0).astype(BF16)
    hb = hmain.astype(BF16)
    sub = tm // PREP_SUBTILES
    gens = [_prep_rows(hext[i * sub:(i + 1) * sub + 2 * HALO], hb[i * sub:(i + 1) * sub], w1_ref, wg_ref, cw_ref,
                       gadd_ref, galog_ref, gqkv_ref.at[0, i * sub:(i + 1) * sub], sub, None)
            for i in range(PREP_SUBTILES)]
    for i, (zo, mqkv, tiles) in enumerate(_drive(gens)):
        zo_ref[0, i * sub:(i + 1) * sub, :] = zo
        mqkv_ref[0, i * sub:(i + 1) * sub, :] = mqkv
        for c in range(sub // CHUNK):
            for k, y in enumerate(tiles):
                rg_ref[0, i * (sub // CHUNK) + c, k * SUBLANES:(k + 1) * SUBLANES, :] = y[:, c * LANES:(c + 1) * LANES]


def _meta_kernel(xh_ref, meta_ref, nw_ref, w1_ref, wg_ref, cw_ref, gadd_ref, galog_ref,
                 s0_ref, c0_ref, m0_ref, gq_ref):
    nw = nw_ref[...]
    npad = META_ROWS - N_META
    hmain = jnp.concatenate([jnp.zeros((npad, D_MODEL), F32), _rms(meta_ref[...], nw)], axis=0)
    hext = jnp.concatenate([jnp.zeros((HALO, D_MODEL), F32), hmain, _rms(xh_ref[0], nw)], axis=0).astype(BF16)
    ((_, mqkv, tiles),) = _drive([_prep_rows(hext, hmain.astype(BF16), w1_ref, wg_ref, cw_ref, gadd_ref, galog_ref,
                                             gq_ref, META_ROWS, npad)])
    last = META_ROWS // CHUNK - 1
    rg = jnp.concatenate([y[:, last * LANES:(last + 1) * LANES] for y in tiles], axis=0)
    state = ([jnp.zeros((GDN_DK, 2 * GDN_DV), F32)] * N_PAIRS,
             [jnp.zeros((MLSTM_DQK, 4 * MLSTM_DV), F32)] * N_PAIRS,
             [jnp.zeros((1, LANES), F32)] * N_PAIRS)
    chunk = (gq_ref[pl.ds(last * CHUNK, CHUNK), :], mqkv[last * CHUNK:], rg)
    (new_state,), _ = _chunk_steps([[chunk]], (0,), [state], False)
    _store_state(new_state, s0_ref.at[0], c0_ref.at[0], m0_ref.at[0])


def _scan_kernel(gqf_ref, mqf_ref, rgf_ref, gqb_ref, mqb_ref, rgb_ref, s0_ref, c0_ref, m0_ref,
                 of_ref, ob_ref, s_ref, c_ref, m_ref, *, cb):
    j = pl.program_id(1)

    @pl.when(j == 0)
    def _():
        s_ref[0] = s0_ref[0]
        c_ref[0] = c0_ref[0]
        m_ref[0] = m0_ref[0]
        s_ref[1] = jnp.zeros(s_ref.shape[1:], F32)
        c_ref[1] = jnp.zeros(c_ref.shape[1:], F32)
        m_ref[1] = jnp.zeros(m_ref.shape[1:], F32)

    chunk = lambda gq, mq, rg, c: (gq[0, c * CHUNK:(c + 1) * CHUNK, :], mq[0, c * CHUNK:(c + 1) * CHUNK, :], rg[0, c])
    order_b = list(range(cb - 1, -1, -1))
    in_f = [chunk(gqf_ref, mqf_ref, rgf_ref, c) for c in range(cb)]
    in_b = [chunk(gqb_ref, mqb_ref, rgb_ref, c) for c in order_b]
    st_f = (s_ref.at[0], c_ref.at[0], m_ref.at[0])
    st_b = (s_ref.at[1], c_ref.at[1], m_ref.at[1])
    (new_f, new_b), (out_f, out_b) = _chunk_steps([in_f, in_b], (0, 1), [_load_state(*st_f), _load_state(*st_b)], True)
    _store_state(new_f, *st_f)
    _store_state(new_b, *st_b)
    for c in range(cb):
        of_ref[0, c * CHUNK:(c + 1) * CHUNK, :] = jnp.concatenate(out_f[c], axis=1).astype(of_ref.dtype)
        cr = order_b[c]
        ob_ref[0, cr * CHUNK:(cr + 1) * CHUNK, :] = jnp.concatenate(out_b[c], axis=1).astype(ob_ref.dtype)


def _ffn_kernel(x_ref, of_ref, ob_ref, zo_ref, hn_ref, wout_ref, n1_ref, n2_ref, wg_ref, wu_ref, wd_ref, n3_ref,
                y_ref, *, ff_chunk):
    hn = hn_ref[...]
    sub = x_ref.shape[0] // FFN_SUBTILES
    tiles = [_Chain() for _ in range(FFN_SUBTILES)]
    ff_slices = [slice(lo, min(lo + ff_chunk, D_FF)) for lo in range(0, D_FF, ff_chunk)]

    def mix_stage(t, rows):
        o = of_ref[rows, :].astype(F32) + ob_ref[rows, :].astype(F32)
        zo = zo_ref[rows, :].astype(F32)
        parts = []
        for h in range(GDN_HEADS + MLSTM_HEADS):
            sl = slice(h * LANES, (h + 1) * LANES)
            parts.append((_rms(o[:, sl], hn[:, sl]) * zo[:, sl]).astype(BF16))
        t.mix = _dot(jnp.concatenate(parts, axis=1), wout_ref[...])

    def norm_stage(t, rows):
        t.x1 = x_ref[rows, :] + _rms(t.mix, n1_ref[...])
        t.h2 = _rms(t.x1, n2_ref[...]).astype(BF16)
        t.f = None

    def ff_stage(c):
        def run(t, rows):
            gate = _dot(t.h2, wg_ref[:, ff_slices[c]])
            up = _dot(t.h2, wu_ref[:, ff_slices[c]])
            act = (gate * jax.nn.sigmoid(gate) * up).astype(BF16)
            part = _dot(act, wd_ref[ff_slices[c], :])
            t.f = part if t.f is None else t.f + part
        return run

    def out_stage(t, rows):
        y_ref[rows, :] = t.x1 + _rms(t.f, n3_ref[...])

    for stage in [mix_stage, norm_stage] + [ff_stage(c) for c in range(len(ff_slices))] + [out_stage]:
        for i, t in enumerate(tiles):
            stage(t, slice(i * sub, (i + 1) * sub))


def _const_spec(shape):
    nd = len(shape)
    return pl.BlockSpec(shape, lambda *_: (0,) * nd, pipeline_mode=pl.Buffered(1))


def _encode(x, p):
    bsz, t, _ = x.shape
    tm = min(512, t)
    nt = t // tm
    cb = min(8, t // CHUNK)
    tb = cb * CHUNK
    nj = t // tb
    nc = t // CHUNK
    cparams = functools.partial(pltpu.CompilerParams, vmem_limit_bytes=VMEM_LIMIT)
    weight_specs = [
        _const_spec((1, D_MODEL)),
        _const_spec((D_MODEL, W1_COLS)),
        _const_spec((N_GATES, D_MODEL)),
        _const_spec((SUBLANES, QKV_W)),
        _const_spec((2 * SUBLANES, LANES)),
        _const_spec((2 * SUBLANES, LANES)),
    ]
    weights = (p["norm_pre_mix"], p["w1"], p["wg"], p["conv_w"], p["gadd"], p["galog"])
    s_shape = (N_PAIRS, GDN_DK, 2 * GDN_DV)
    c_shape = (N_PAIRS, MLSTM_DQK, 4 * MLSTM_DV)
    m_shape = (N_PAIRS, SUBLANES, LANES)
    state_specs = [pl.BlockSpec((1,) + s, lambda b, *_: (b, 0, 0, 0)) for s in (s_shape, c_shape, m_shape)]

    s0, c0, m0 = pl.pallas_call(
        _meta_kernel,
        grid=(bsz,),
        in_specs=[pl.BlockSpec((1, HALO, D_MODEL), lambda b: (b, 0, 0)),
                  _const_spec((N_META, D_MODEL))] + weight_specs,
        out_specs=state_specs,
        out_shape=[jax.ShapeDtypeStruct((bsz,) + s, F32) for s in (s_shape, c_shape, m_shape)],
        scratch_shapes=[
            pltpu.VMEM((META_ROWS, QKV_W), BF16),
        ],
        compiler_params=cparams(dimension_semantics=("arbitrary",)),
        name="meta_state",
    )(x, p["meta"], *weights)

    tpb = tm // HALO
    gqkv, zo, mqkv, rg = pl.pallas_call(
        functools.partial(_prep_kernel, tm=tm),
        grid=(bsz, nt),
        in_specs=[
            pl.BlockSpec((1, tm, D_MODEL), lambda b, i: (b, i, 0)),
            pl.BlockSpec((1, HALO, D_MODEL), lambda b, i: (b, jnp.maximum(i * tpb - 1, 0), 0)),
            pl.BlockSpec((1, HALO, D_MODEL), lambda b, i: (b, jnp.minimum((i + 1) * tpb, t // HALO - 1), 0)),
            pl.BlockSpec((HALO, D_MODEL), lambda b, i: (N_META // HALO - 1, 0)),
        ] + weight_specs,
        out_specs=[
            pl.BlockSpec((1, tm, QKV_W), lambda b, i: (b, i, 0)),
            pl.BlockSpec((1, tm, ZO_W), lambda b, i: (b, i, 0)),
            pl.BlockSpec((1, tm, MQKV_W), lambda b, i: (b, i, 0)),
            pl.BlockSpec((1, tm // CHUNK, RG_ROWS, LANES), lambda b, i: (b, i, 0, 0)),
        ],
        out_shape=[
            jax.ShapeDtypeStruct((bsz, t, QKV_W), BF16),
            jax.ShapeDtypeStruct((bsz, t, ZO_W), BF16),
            jax.ShapeDtypeStruct((bsz, t, MQKV_W), BF16),
            jax.ShapeDtypeStruct((bsz, nc, RG_ROWS, LANES), F32),
        ],
        compiler_params=cparams(dimension_semantics=("arbitrary", "arbitrary")),
        name="token_prep",
    )(x, x, x, p["meta"], *weights)

    fwd = lambda b, j: (b, j, 0)
    bwd = lambda b, j: (b, nj - 1 - j, 0)
    fwd4 = lambda b, j: (b, j, 0, 0)
    bwd4 = lambda b, j: (b, nj - 1 - j, 0, 0)
    o_f, o_b = pl.pallas_call(
        functools.partial(_scan_kernel, cb=cb),
        grid=(bsz, nj),
        in_specs=[
            pl.BlockSpec((1, tb, QKV_W), fwd),
            pl.BlockSpec((1, tb, MQKV_W), fwd),
            pl.BlockSpec((1, cb, RG_ROWS, LANES), fwd4),
            pl.BlockSpec((1, tb, QKV_W), bwd),
            pl.BlockSpec((1, tb, MQKV_W), bwd),
            pl.BlockSpec((1, cb, RG_ROWS, LANES), bwd4),
        ] + state_specs,
        out_specs=[
            pl.BlockSpec((1, tb, ZO_W), fwd),
            pl.BlockSpec((1, tb, ZO_W), bwd),
        ],
        out_shape=[
            jax.ShapeDtypeStruct((bsz, t, ZO_W), BF16),
            jax.ShapeDtypeStruct((bsz, t, ZO_W), BF16),
        ],
        scratch_shapes=[
            pltpu.VMEM((2,) + s_shape, F32),
            pltpu.VMEM((2,) + c_shape, F32),
            pltpu.VMEM((2,) + m_shape, F32),
        ],
        compiler_params=cparams(dimension_semantics=("arbitrary", "arbitrary")),
        name="chunk_scan",
    )(gqkv, mqkv, rg, gqkv, mqkv, rg, s0, c0, m0)

    rows = bsz * t
    tr = min(512, rows)
    row_spec = lambda w: pl.BlockSpec((tr, w), lambda i: (i, 0))
    y = pl.pallas_call(
        functools.partial(_ffn_kernel, ff_chunk=FF_CHUNK),
        grid=(rows // tr,),
        in_specs=[
            row_spec(D_MODEL), row_spec(ZO_W), row_spec(ZO_W), row_spec(ZO_W),
            _const_spec((1, ZO_W)),
            _const_spec((ZO_W, D_MODEL)),
            _const_spec((1, D_MODEL)),
            _const_spec((1, D_MODEL)),
            _const_spec((D_MODEL, D_FF)),
            _const_spec((D_MODEL, D_FF)),
            _const_spec((D_FF, D_MODEL)),
            _const_spec((1, D_MODEL)),
        ],
        out_specs=row_spec(D_MODEL),
        out_shape=jax.ShapeDtypeStruct((rows, D_MODEL), F32),
        compiler_params=cparams(dimension_semantics=("arbitrary",)),
        name="mix_ffn",
    )(x.reshape(rows, D_MODEL), o_f.reshape(rows, ZO_W), o_b.reshape(rows, ZO_W), zo.reshape(rows, ZO_W),
      p["head_norm"], p["w_out"], p["norm_post_mix"], p["norm_pre_ffn"],
      p["w_gate"], p["w_up"], p["w_down"], p["norm_post_ffn"])
    return y.reshape(bsz, t, D_MODEL)


def _pair_rows(v):
    return jnp.repeat(v.reshape(2 * N_PAIRS, 2), CHUNK, axis=1)


def _prepare_params(meta_tokens, norm_pre_mix, w_in, conv_w, A_log, dt_bias, gdn_norm, i_bias, f_bias,
                    mlstm_norm, w_out, norm_post_mix, norm_pre_ffn, w_gate, w_up, w_down, norm_post_ffn):
    w = w_in[0]
    o_qkv = 0
    o_z = o_qkv + QKV_W
    o_a = o_z + GDN_HEADS * GDN_DV
    o_b = o_a + 2 * GDN_HEADS
    o_mq = o_b + 2 * GDN_HEADS
    o_o = o_mq + MQKV_W
    o_i = o_o + MLSTM_HEADS * MLSTM_DV
    o_f = o_i + 2 * MLSTM_HEADS
    w1 = jnp.concatenate([w[:, o_qkv:o_z], w[:, o_z:o_a], w[:, o_o:o_i], w[:, o_mq:o_o]], axis=1).astype(BF16)
    gcols = jnp.stack([w[:, o:o + 2 * GDN_HEADS] for o in (o_a, o_b, o_i, o_f)], axis=1)
    gcols = gcols.reshape(D_MODEL, 4, 2, N_PAIRS, 2)
    wg = jnp.transpose(gcols, (4, 1, 2, 3, 0)).reshape(N_GATES, D_MODEL).astype(BF16)
    zeros4 = jnp.zeros((2 * N_PAIRS, LANES), F32)
    gadd = jnp.concatenate([_pair_rows(dt_bias[0]), zeros4, _pair_rows(i_bias[0]), _pair_rows(f_bias[0])], axis=0)
    galog = jnp.concatenate([_pair_rows(A_log[0]), zeros4, zeros4, zeros4], axis=0)
    cw = jnp.zeros((SUBLANES, QKV_W), F32).at[0:CONV_K].set(conv_w[0])
    head_norm = jnp.concatenate([jnp.tile(gdn_norm[0], GDN_HEADS), mlstm_norm[0]]).reshape(1, ZO_W)
    return {
        "meta": meta_tokens,
        "norm_pre_mix": norm_pre_mix[0].reshape(1, D_MODEL),
        "w1": w1,
        "wg": wg,
        "conv_w": cw,
        "gadd": gadd,
        "galog": galog,
        "head_norm": head_norm,
        "w_out": w_out[0].astype(BF16),
        "norm_post_mix": norm_post_mix[0].reshape(1, D_MODEL),
        "norm_pre_ffn": norm_pre_ffn[0].reshape(1, D_MODEL),
        "w_gate": w_gate[0].astype(BF16),
        "w_up": w_up[0].astype(BF16),
        "w_down": w_down[0].astype(BF16),
        "norm_post_ffn": norm_post_ffn[0].reshape(1, D_MODEL),
    }


def kernel(x_prompt, x_sample, meta_tokens, norm_pre_mix, w_in, conv_w, A_log, dt_bias, gdn_norm, i_bias, f_bias,
           mlstm_norm, w_out, norm_post_mix, norm_pre_ffn, w_gate, w_up, w_down, norm_post_ffn):
    p = _prepare_params(meta_tokens, norm_pre_mix, w_in, conv_w, A_log, dt_bias, gdn_norm, i_bias, f_bias,
                        mlstm_norm, w_out, norm_post_mix, norm_pre_ffn, w_gate, w_up, w_down, norm_post_ffn)
    return (_encode(x_prompt, p), _encode(x_sample, p))
```

```python
import functools

import jax
import jax.numpy as jnp
from jax import lax
from jax.experimental import pallas as pl
from jax.experimental.pallas import tpu as pltpu

D_MODEL = 1024
N_META = 16
CHUNK = 64
CONV_K = 5
EPS = 1e-6
GDN_HEADS = 4
GDN_DK = 128
GDN_DV = 128
MLSTM_HEADS = 4
MLSTM_DQK = 64
MLSTM_DV = 128
D_FF = 2816
N_PAIRS = 2
QKV_W = 3 * GDN_HEADS * GDN_DK
ZO_W = 2 * GDN_HEADS * GDN_DV
MQKV_W = 2 * MLSTM_HEADS * MLSTM_DQK + MLSTM_HEADS * MLSTM_DV
W1_COLS = QKV_W + ZO_W + MQKV_W
N_GATES = 4 * 2 * GDN_HEADS
LANES = 128
SUBLANES = 8
HALO = SUBLANES
CONV_ROWS = 128
META_ROWS = 2 * CHUNK
FFN_SUBTILES = 4
PREP_SUBTILES = 4
PREP_SUB_ROWS = 256
MXU_WIDTH = 256
FF_CHUNK = 6 * MXU_WIDTH
NEG_BIG = -1e30
VMEM_LIMIT = 56 * 1024 * 1024

R_GC, R_BETA, R_EG, R_EDB, R_R, R_CM, R_ET0, R_ET1, R_CML, R_BL, R_B = 0, 4, 8, 12, 16, 20, 24, 28, 32, 36, 44
RG_ROWS = 48

F32 = jnp.float32
BF16 = jnp.bfloat16


def _dot(a, b):
    return jnp.dot(a, b, preferred_element_type=F32)


def _dot_nt(a, b):
    return lax.dot_general(a, b, (((1,), (1,)), ((), ())), preferred_element_type=F32)


def _rms(v, w):
    ms = jnp.mean(v * v, axis=-1, keepdims=True)
    return v * lax.rsqrt(ms + EPS) * w


def _softplus(v):
    return jnp.maximum(v, 0.0) + jnp.log(1.0 + jnp.exp(-jnp.abs(v)))


def _lane_scan(cur, op, forward, lanemod):
    n = cur.shape[1]
    k = 1
    while k < CHUNK:
        if forward:
            shifted = pltpu.roll(cur, k, 1)
            ok = lanemod >= k
        else:
            shifted = pltpu.roll(cur, n - k, 1)
            ok = lanemod < CHUNK - k
        cur = jnp.where(ok, op(cur, shifted), cur)
        k *= 2
    return cur


def _pair_layout(a, b):
    tm = a.shape[1]
    low = (lax.broadcasted_iota(jnp.int32, (SUBLANES, tm), 1) & (LANES - 1)) < CHUNK
    even = jnp.where(low, a, pltpu.roll(b, CHUNK, 1))
    odd = jnp.where(low, pltpu.roll(a, tm - CHUNK, 1), b)
    cols = []
    for v in range(tm // LANES):
        cols += [even[:, v * LANES:(v + 1) * LANES], odd[:, v * LANES:(v + 1) * LANES]]
    return jnp.concatenate(cols, axis=1)


def _gate_tiles(gp, gadd_ref, galog_ref, tm, first_valid):
    n = 2 * tm
    top = _pair_layout(gp[0:8], gp[16:24])
    bot = _pair_layout(gp[8:16], gp[24:32])
    row = lax.broadcasted_iota(jnp.int32, (SUBLANES, n), 0)
    lane = lax.broadcasted_iota(jnp.int32, (SUBLANES, n), 1)
    first4 = row < 4
    backward = (row & 2) != 0
    low = (lane & (LANES - 1)) < CHUNK
    lanemod = lane & (CHUNK - 1)
    reps = n // LANES
    gadd = gadd_ref[...]
    ytop = top + jnp.concatenate([gadd[0:8]] * reps, axis=1)
    ybot = bot + jnp.concatenate([gadd[8:16]] * reps, axis=1)
    neg_a = -jnp.exp(jnp.concatenate([galog_ref[0:8, :]] * reps, axis=1))
    etop = jnp.where(first4, neg_a * _softplus(ytop), jax.nn.sigmoid(ytop))
    ebot = jnp.where(first4, ybot, -_softplus(-ybot))
    if first_valid is not None:
        token = lax.shift_right_logical(lane, LANES.bit_length() - 1) * CHUNK + lanemod
        ok = token >= first_valid
        etop = jnp.where(ok, etop, 0.0)
        ebot = jnp.where(ok, ebot, jnp.where(first4, NEG_BIG, 0.0))
    add = lambda a, b: a + b
    swap = lambda a: pltpu.roll(a, 4, 0)

    pre = _lane_scan(etop, add, True, lanemod)
    suf = _lane_scan(etop, add, False, lanemod)
    gc = jnp.where(backward, suf, pre)
    tot = pre + suf - etop
    eg = jnp.exp(gc)
    edb = jnp.exp(tot - gc) * swap(etop)
    etot = jnp.exp(tot)
    et0 = jnp.where(low, etot, pltpu.roll(etot, CHUNK, 1))
    et1 = jnp.where(low, pltpu.roll(etot, n - CHUNK, 1), etot)
    y1 = jnp.where(first4, gc, etop)
    y2 = jnp.where(first4, eg, swap(edb))
    y4 = jnp.where(first4, et0, swap(et1))

    pre = _lane_scan(ebot, add, True, lanemod)
    suf = _lane_scan(ebot, add, False, lanemod)
    bsum = jnp.where(backward, suf, pre)
    btot = pre + suf - ebot
    r = ebot - swap(bsum)
    pmax = _lane_scan(r, jnp.maximum, True, lanemod)
    smax = _lane_scan(r, jnp.maximum, False, lanemod)
    cm = jnp.where(backward, smax, pmax)
    y3 = jnp.where(first4, r, swap(cm))
    y5 = jnp.where(first4, jnp.maximum(pmax, smax), btot)
    return [y1, y2, y3, y4, y5, bsum]


def _prep_rows(hext, hb, w1_ref, wg_ref, cw_ref, gadd_ref, galog_ref, gq_out, tm, first_valid):
    gp = _dot_nt(wg_ref[...], hb)
    ext = _dot(hext, w1_ref[:, 0:QKV_W])
    c0 = QKV_W + ZO_W
    pz = _dot(hb, w1_ref[:, QKV_W:QKV_W + ZO_W // 2])
    po = _dot(hb, w1_ref[:, QKV_W + ZO_W // 2:QKV_W + ZO_W])
    pm = _dot(hb, w1_ref[:, c0:c0 + MQKV_W])
    yield
    tiles = _gate_tiles(gp, gadd_ref, galog_ref, tm, first_valid)
    nrow = tm + 2 * HALO
    taps = [pltpu.roll(ext, ((CONV_K - 1) // 2 - j) % nrow, 0) if j != (CONV_K - 1) // 2 else ext
            for j in range(CONV_K)]

    conv_rows = min(CONV_ROWS, tm)
    for r0 in range(0, tm, conv_rows):
        acc = None
        for j in range(CONV_K):
            term = cw_ref[j:j + 1, :] * taps[j][HALO + r0:HALO + r0 + conv_rows]
            acc = term if acc is None else acc + term
        qkv = acc * jax.nn.sigmoid(acc)
        if first_valid is not None:
            rows = r0 + lax.broadcasted_iota(jnp.int32, (conv_rows, 1), 0)
            qkv = jnp.where(rows >= first_valid, qkv, 0.0)
        parts = []
        for idx in range(2 * GDN_HEADS):
            xh = qkv[:, idx * GDN_DK:(idx + 1) * GDN_DK]
            scale = lax.rsqrt(jnp.sum(xh * xh, axis=-1, keepdims=True) + EPS)
            if idx < GDN_HEADS:
                scale = scale * (GDN_DK ** -0.5)
            parts.append((xh * scale).astype(BF16))
        parts.append(qkv[:, 2 * GDN_HEADS * GDN_DK:].astype(BF16))
        gq_out[pl.ds(r0, conv_rows), :] = jnp.concatenate(parts, axis=1)

    yield
    zo = jnp.concatenate([(pz * jax.nn.sigmoid(pz)).astype(BF16),
                          jax.nn.sigmoid(po).astype(BF16)], axis=1)
    nq = MLSTM_HEADS * MLSTM_DQK
    mqkv = jnp.concatenate([(pm[:, 0:nq] * (MLSTM_DQK ** -0.5)).astype(BF16),
                            pm[:, nq:].astype(BF16)], axis=1)
    return zo, mqkv, tiles


def _drive(gens):
    results = [None] * len(gens)
    live = list(range(len(gens)))
    while live:
        for i in list(live):
            try:
                next(gens[i])
            except StopIteration as stop:
                results[i] = stop.value
                live.remove(i)
    return results


class _Chain:
    pass


def _bd(x, half):
    lane = lax.broadcasted_iota(jnp.int32, x.shape, 1)
    zero = jnp.zeros_like(x)
    return jnp.concatenate([jnp.where(lane < half, x, zero), jnp.where(lane >= half, x, zero)], axis=0)


def _rows_to_cols(rows):
    pad = jnp.zeros((SUBLANES - len(rows), LANES), F32)
    return jnp.concatenate(list(rows) + [pad], axis=0).T


def _pair_cols(cols, j, low):
    first = jnp.broadcast_to(cols[:CHUNK, j:j + 1], (CHUNK, LANES))
    second = jnp.broadcast_to(cols[CHUNK:, j:j + 1], (CHUNK, LANES))
    return jnp.where(low, first, second)


def _chunk_steps(inputs, dirs, states, want_out):
    ti = lax.broadcasted_iota(jnp.int32, (CHUNK, LANES), 0)
    si = lax.broadcasted_iota(jnp.int32, (CHUNK, LANES), 1) & (CHUNK - 1)
    eye2 = jnp.where(ti == si, 1.0, 0.0)
    low = lax.broadcasted_iota(jnp.int32, (CHUNK, LANES), 1) < CHUNK
    low_row = lax.broadcasted_iota(jnp.int32, (1, LANES), 1) < CHUNK
    ones_v = jnp.ones((CHUNK, MLSTM_DV), BF16)
    nq = MLSTM_HEADS * MLSTM_DQK

    gd, ml = [], []
    steps = []
    m_final = []
    for d, chunks, (_, _, m_in) in zip(dirs, inputs, states):
        incl = ti >= si if d == 0 else ti <= si
        strict = ti > si if d == 0 else ti < si
        m_run = list(m_in)
        per_chunk = []
        for gq, mq, rg in chunks:
            g_step, m_step = [], []
            for p in range(N_PAIRS):
                row = lambda base: rg[base + 2 * d + p:base + 2 * d + p + 1, :]
                c = _Chain()
                c.incl, c.strict = incl, strict
                c.q = gq[:, 2 * GDN_DK * p:2 * GDN_DK * (p + 1)]
                c.k = gq[:, GDN_HEADS * GDN_DK + 2 * GDN_DK * p:GDN_HEADS * GDN_DK + 2 * GDN_DK * (p + 1)]
                c.v = gq[:, 2 * GDN_HEADS * GDN_DK + 2 * GDN_DV * p:2 * GDN_HEADS * GDN_DK + 2 * GDN_DV * (p + 1)]
                c.gc, c.beta, c.eg, c.edb = row(R_GC), row(R_BETA), row(R_EG), row(R_EDB)
                c.etot = jnp.concatenate([row(R_ET0), row(R_ET1)], axis=1)
                g_step.append(c)
                c = _Chain()
                c.incl = incl
                c.q = mq[:, 2 * MLSTM_DQK * p:2 * MLSTM_DQK * (p + 1)]
                c.k = mq[:, nq + 2 * MLSTM_DQK * p:nq + 2 * MLSTM_DQK * (p + 1)]
                v = mq[:, 2 * nq + 2 * MLSTM_DV * p:2 * nq + 2 * MLSTM_DV * (p + 1)]
                c.va = jnp.concatenate([v[:, :MLSTM_DV], ones_v, v[:, MLSTM_DV:], ones_v], axis=1)
                c.m = m_run[p]
                c.r, c.b = row(R_R), row(R_B)
                c.mt = jnp.maximum(c.m, row(R_CM))
                c.ml = jnp.maximum(c.m, row(R_CML))
                m_run[p] = row(R_BL) + c.ml
                m_step.append(c)
            per_chunk.append((g_step, m_step))
            gd += g_step
            ml += m_step
        steps.append(per_chunk)
        m_final.append(m_run)

    def stage_qk(gd, ml):
        for c, cm in zip(gd, ml):
            cols = _rows_to_cols([c.gc, cm.mt, -(cm.b + cm.mt)])
            c.gccol = _pair_cols(cols, 0, low)
            cm.mtcol = _pair_cols(cols, 1, low)
            if want_out:
                cm.emt = [jnp.exp(jnp.broadcast_to(cols[:CHUNK, 2:3], (CHUNK, LANES))),
                          jnp.exp(jnp.broadcast_to(cols[CHUNK:, 2:3], (CHUNK, LANES)))]
        for c in gd:
            c.kmask = _bd(c.k, GDN_DK)
            kst = jnp.concatenate([c.k[:, :GDN_DK], c.k[:, GDN_DK:]], axis=0)
            c.kt = kst.T
            kq = _dot(jnp.concatenate([c.q, c.k], axis=0), _bd(c.kt, CHUNK))
            c.qk, c.kk = kq[:CHUNK], kq[CHUNK:]
        for c in ml:
            c.kt = c.k.T
            kt2 = jnp.concatenate([c.kt, c.kt], axis=1)
            rowh = lax.broadcasted_iota(jnp.int32, (2 * MLSTM_DQK, LANES), 0) < MLSTM_DQK
            laneh = lax.broadcasted_iota(jnp.int32, (2 * MLSTM_DQK, LANES), 1) < CHUNK
            c.qk = _dot(c.q, jnp.where(rowh == laneh, kt2, jnp.zeros_like(kt2)))

    def stage_a(gd, ml):
        for c in gd:
            decay = jnp.where(c.incl, jnp.exp(jnp.where(c.incl, c.gccol - c.gc, 0.0)), 0.0)
            a = jnp.where(c.strict, c.kk * decay, 0.0) * c.beta
            c.aqkb = c.qk * decay * c.beta
            c.p = eye2 - a
            ab = a.astype(BF16)
            c.b = _dot(ab, _bd(ab, CHUNK))
        for c in ml:
            w_row = jnp.exp(c.r - c.ml)
            dec = jnp.exp(c.m - c.ml)
            d0 = jnp.where(low_row, dec, pltpu.roll(dec, CHUNK, 1))
            d1 = jnp.where(low_row, pltpu.roll(dec, CHUNK, 1), dec)
            w1 = pltpu.roll(w_row, CHUNK, 1)
            wmat = jnp.concatenate([jnp.broadcast_to(w_row[:, :CHUNK], (MLSTM_DQK, CHUNK)),
                                    jnp.broadcast_to(w1[:, :CHUNK], (MLSTM_DQK, CHUNK))], axis=0)
            ktw = (c.kt.astype(F32) * wmat).astype(BF16)
            c.upd = jnp.concatenate([_dot(ktw[:MLSTM_DQK], c.va[:, :2 * MLSTM_DV]),
                                     _dot(ktw[MLSTM_DQK:], c.va[:, 2 * MLSTM_DV:])], axis=1)
            c.dec = jnp.concatenate([d0, d0, d1, d1], axis=1)

    def stage_power(gd, ml):
        for c in gd:
            bb = c.b.astype(BF16)
            x = _dot(jnp.concatenate([c.p.astype(BF16), bb], axis=0), _bd(bb, CHUNK))
            c.p = c.p + x[:CHUNK]
            c.b = x[CHUNK:]

    def stage_inverse(gd, ml):
        if want_out:
            for c in ml:
                dm = jnp.where(c.incl, jnp.exp(jnp.where(c.incl, c.r - c.mtcol, 0.0)), 0.0)
                c.lhs_t = jnp.concatenate([c.qk * dm, eye2 * jnp.exp(c.m - c.mt)], axis=1).astype(BF16)
                c.va_bd = _bd(c.va, 2 * MLSTM_DV)
        for c in gd:
            c.t = c.p + _dot(c.p.astype(BF16), _bd(c.b.astype(BF16), CHUNK))

    def stage_uw(gd, ml):
        for c in gd:
            c.ut = _dot(c.t.astype(BF16), _bd(c.v, GDN_DV))
            wt = _dot((c.t * c.eg).astype(BF16), c.kmask)
            c.lhs_s = jnp.concatenate([wt.astype(BF16), c.q], axis=0)
            c.kdt = (c.kt.astype(F32) * c.edb).astype(BF16)
            if want_out:
                c.lhs_o = jnp.concatenate([c.aqkb, eye2 * c.eg], axis=1).astype(BF16)

    n_levels = 0
    n = 2
    while 2 * n < CHUNK:
        n_levels += 1
        n *= 2
    stages = [stage_qk, stage_a] + [stage_power] * n_levels + [stage_inverse, stage_uw]

    s_run = [list(st[0]) for st in states]
    c_run = [list(st[1]) for st in states]
    outs = [[] for _ in dirs]

    def state_step(ci):
        now = [(i, p, steps[i][ci][0][p], steps[i][ci][1][p]) for i in range(len(dirs)) for p in range(N_PAIRS)]
        for i, p, g, m in now:
            g.wq = _dot(g.lhs_s, _bd(s_run[i][p].astype(BF16), GDN_DV))
        if want_out:
            for i, p, g, m in now:
                m.qc = _dot(m.q, _bd(c_run[i][p].astype(BF16), 2 * MLSTM_DV))
        for i, p, g, m in now:
            vbd = _bd((g.ut - g.wq[:CHUNK]).astype(BF16), GDN_DV)
            s_run[i][p] = s_run[i][p] * g.etot + _dot(g.kdt, vbd)
            if want_out:
                rhs = jnp.concatenate([vbd, _bd(g.wq[CHUNK:].astype(BF16), GDN_DV)], axis=0)
                g.o = _dot(g.lhs_o, rhs)
        for i, p, g, m in now:
            if want_out:
                rhs = jnp.concatenate([m.va_bd, _bd(m.qc.astype(BF16), 2 * MLSTM_DV)], axis=0)
                m.tot = _dot(m.lhs_t, rhs)
            c_run[i][p] = c_run[i][p] * m.dec + m.upd
        if want_out:
            for i in range(len(dirs)):
                pieces = [g.o for _, _, g, _ in now[i * N_PAIRS:(i + 1) * N_PAIRS]]
                for _, _, _, m in now[i * N_PAIRS:(i + 1) * N_PAIRS]:
                    for h in range(2):
                        num = m.tot[:, 2 * MLSTM_DV * h:2 * MLSTM_DV * h + MLSTM_DV]
                        den = m.tot[:, 2 * MLSTM_DV * h + MLSTM_DV:2 * MLSTM_DV * (h + 1)]
                        pieces.append(num / jnp.maximum(jnp.abs(den), m.emt[h]))
                outs[i].append(pieces)

    n_chunks = len(inputs[0])
    for step in range(n_chunks + len(stages)):
        for ci in range(n_chunks):
            s = step - ci
            if 0 <= s < len(stages):
                stages[s]([g for st in steps for g in st[ci][0]], [m for st in steps for m in st[ci][1]])
            elif s == len(stages):
                state_step(ci)
    return [(s_run[i], c_run[i], m_final[i]) for i in range(len(dirs))], outs


def _load_state(s_ref, c_ref, m_ref):
    return ([s_ref[p] for p in range(N_PAIRS)], [c_ref[p] for p in range(N_PAIRS)],
            [m_ref[p][0:1, :] for p in range(N_PAIRS)])


def _store_state(state, s_ref, c_ref, m_ref):
    s_out, c_out, m_out = state
    for p in range(N_PAIRS):
        s_ref[p] = s_out[p]
        c_ref[p] = c_out[p]
        m_ref[p] = jnp.broadcast_to(m_out[p], (SUBLANES, LANES))


def _prep_kernel(x_ref, xl_ref, xr_ref, meta_ref, nw_ref, w1_ref, wg_ref, cw_ref, gadd_ref, galog_ref,
                 gqkv_ref, zo_ref, mqkv_ref, rg_ref, *, tm):
    i = pl.program_id(1)
    last = pl.num_programs(1) - 1
    nw = nw_ref[...]
    left = jnp.where(i == 0, meta_ref[...], xl_ref[0])
    right = xr_ref[0] * (i < last).astype(F32)
    hmain = _rms(x_ref[0], nw)
    hext = jnp.concatenate([_rms(left, nw), hmain, _rms(right, nw)], axis=0).astype(BF16)
    hb = hmain.astype(BF16)
    sub = tm // PREP_SUBTILES
    gens = [_prep_rows(hext[i * sub:(i + 1) * sub + 2 * HALO], hb[i * sub:(i + 1) * sub], w1_ref, wg_ref, cw_ref,
                       gadd_ref, galog_ref, gqkv_ref.at[0, i * sub:(i + 1) * sub], sub, None)
            for i in range(PREP_SUBTILES)]
    for i, (zo, mqkv, tiles) in enumerate(_drive(gens)):
        zo_ref[0, i * sub:(i + 1) * sub, :] = zo
        mqkv_ref[0, i * sub:(i + 1) * sub, :] = mqkv
        for c in range(sub // CHUNK):
            for k, y in enumerate(tiles):
                rg_ref[0, i * (sub // CHUNK) + c, k * SUBLANES:(k + 1) * SUBLANES, :] = y[:, c * LANES:(c + 1) * LANES]


def _meta_kernel(xh_ref, meta_ref, nw_ref, w1_ref, wg_ref, cw_ref, gadd_ref, galog_ref,
                 s0_ref, c0_ref, m0_ref, gq_ref):
    nw = nw_ref[...]
    npad = META_ROWS - N_META
    hmain = jnp.concatenate([jnp.zeros((npad, D_MODEL), F32), _rms(meta_ref[...], nw)], axis=0)
    hext = jnp.concatenate([jnp.zeros((HALO, D_MODEL), F32), hmain, _rms(xh_ref[0], nw)], axis=0).astype(BF16)
    ((_, mqkv, tiles),) = _drive([_prep_rows(hext, hmain.astype(BF16), w1_ref, wg_ref, cw_ref, gadd_ref, galog_ref,
                                             gq_ref, META_ROWS, npad)])
    last = META_ROWS // CHUNK - 1
    rg = jnp.concatenate([y[:, last * LANES:(last + 1) * LANES] for y in tiles], axis=0)
    state = ([jnp.zeros((GDN_DK, 2 * GDN_DV), F32)] * N_PAIRS,
             [jnp.zeros((MLSTM_DQK, 4 * MLSTM_DV), F32)] * N_PAIRS,
             [jnp.zeros((1, LANES), F32)] * N_PAIRS)
    chunk = (gq_ref[pl.ds(last * CHUNK, CHUNK), :], mqkv[last * CHUNK:], rg)
    (new_state,), _ = _chunk_steps([[chunk]], (0,), [state], False)
    _store_state(new_state, s0_ref.at[0], c0_ref.at[0], m0_ref.at[0])


def _scan_kernel(gqf_ref, mqf_ref, rgf_ref, gqb_ref, mqb_ref, rgb_ref, s0_ref, c0_ref, m0_ref,
                 of_ref, ob_ref, s_ref, c_ref, m_ref, *, cb):
    j = pl.program_id(1)

    @pl.when(j == 0)
    def _():
        s_ref[0] = s0_ref[0]
        c_ref[0] = c0_ref[0]
        m_ref[0] = m0_ref[0]
        s_ref[1] = jnp.zeros(s_ref.shape[1:], F32)
        c_ref[1] = jnp.zeros(c_ref.shape[1:], F32)
        m_ref[1] = jnp.zeros(m_ref.shape[1:], F32)

    chunk = lambda gq, mq, rg, c: (gq[0, c * CHUNK:(c + 1) * CHUNK, :], mq[0, c * CHUNK:(c + 1) * CHUNK, :], rg[0, c])
    order_b = list(range(cb - 1, -1, -1))
    in_f = [chunk(gqf_ref, mqf_ref, rgf_ref, c) for c in range(cb)]
    in_b = [chunk(gqb_ref, mqb_ref, rgb_ref, c) for c in order_b]
    st_f = (s_ref.at[0], c_ref.at[0], m_ref.at[0])
    st_b = (s_ref.at[1], c_ref.at[1], m_ref.at[1])
    (new_f, new_b), (out_f, out_b) = _chunk_steps([in_f, in_b], (0, 1), [_load_state(*st_f), _load_state(*st_b)], True)
    _store_state(new_f, *st_f)
    _store_state(new_b, *st_b)
    for c in range(cb):
        of_ref[0, c * CHUNK:(c + 1) * CHUNK, :] = jnp.concatenate(out_f[c], axis=1).astype(of_ref.dtype)
        cr = order_b[c]
        ob_ref[0, cr * CHUNK:(cr + 1) * CHUNK, :] = jnp.concatenate(out_b[c], axis=1).astype(ob_ref.dtype)


def _ffn_kernel(x_ref, of_ref, ob_ref, zo_ref, hn_ref, wout_ref, n1_ref, n2_ref, wg_ref, wu_ref, wd_ref, n3_ref,
                y_ref, *, ff_chunk):
    hn = hn_ref[...]
    sub = x_ref.shape[0] // FFN_SUBTILES
    tiles = [_Chain() for _ in range(FFN_SUBTILES)]
    ff_slices = [slice(lo, min(lo + ff_chunk, D_FF)) for lo in range(0, D_FF, ff_chunk)]

    def mix_stage(t, rows):
        o = of_ref[rows, :].astype(F32) + ob_ref[rows, :].astype(F32)
        zo = zo_ref[rows, :].astype(F32)
        parts = []
        for h in range(GDN_HEADS + MLSTM_HEADS):
            sl = slice(h * LANES, (h + 1) * LANES)
            parts.append((_rms(o[:, sl], hn[:, sl]) * zo[:, sl]).astype(BF16))
        t.mix = _dot(jnp.concatenate(parts, axis=1), wout_ref[...])

    def norm_stage(t, rows):
        t.x1 = x_ref[rows, :] + _rms(t.mix, n1_ref[...])
        t.h2 = _rms(t.x1, n2_ref[...]).astype(BF16)
        t.f = None

    def ff_stage(c):
        def run(t, rows):
            gate = _dot(t.h2, wg_ref[:, ff_slices[c]])
            up = _dot(t.h2, wu_ref[:, ff_slices[c]])
            act = (gate * jax.nn.sigmoid(gate) * up).astype(BF16)
            part = _dot(act, wd_ref[ff_slices[c], :])
            t.f = part if t.f is None else t.f + part
        return run

    def out_stage(t, rows):
        y_ref[rows, :] = t.x1 + _rms(t.f, n3_ref[...])

    stages = [mix_stage, norm_stage] + [ff_stage(c) for c in range(len(ff_slices))] + [out_stage]
    for step in range(len(stages) + FFN_SUBTILES - 1):
        for i, t in enumerate(tiles):
            if 0 <= step - i < len(stages):
                stages[step - i](t, slice(i * sub, (i + 1) * sub))


def _const_spec(shape):
    nd = len(shape)
    return pl.BlockSpec(shape, lambda *_: (0,) * nd, pipeline_mode=pl.Buffered(1))


def _encode(x, p):
    bsz, t, _ = x.shape
    tm = min(PREP_SUBTILES * PREP_SUB_ROWS, t)
    nt = t // tm
    cb = min(16, t // CHUNK)
    tb = cb * CHUNK
    nj = t // tb
    nc = t // CHUNK
    cparams = functools.partial(pltpu.CompilerParams, vmem_limit_bytes=VMEM_LIMIT)
    weight_specs = [
        _const_spec((1, D_MODEL)),
        _const_spec((D_MODEL, W1_COLS)),
        _const_spec((N_GATES, D_MODEL)),
        _const_spec((SUBLANES, QKV_W)),
        _const_spec((2 * SUBLANES, LANES)),
        _const_spec((2 * SUBLANES, LANES)),
    ]
    weights = (p["norm_pre_mix"], p["w1"], p["wg"], p["conv_w"], p["gadd"], p["galog"])
    s_shape = (N_PAIRS, GDN_DK, 2 * GDN_DV)
    c_shape = (N_PAIRS, MLSTM_DQK, 4 * MLSTM_DV)
    m_shape = (N_PAIRS, SUBLANES, LANES)
    state_specs = [pl.BlockSpec((1,) + s, lambda b, *_: (b, 0, 0, 0)) for s in (s_shape, c_shape, m_shape)]

    s0, c0, m0 = pl.pallas_call(
        _meta_kernel,
        grid=(bsz,),
        in_specs=[pl.BlockSpec((1, HALO, D_MODEL), lambda b: (b, 0, 0)),
                  _const_spec((N_META, D_MODEL))] + weight_specs,
        out_specs=state_specs,
        out_shape=[jax.ShapeDtypeStruct((bsz,) + s, F32) for s in (s_shape, c_shape, m_shape)],
        scratch_shapes=[
            pltpu.VMEM((META_ROWS, QKV_W), BF16),
        ],
        compiler_params=cparams(dimension_semantics=("arbitrary",)),
        name="meta_state",
    )(x, p["meta"], *weights)

    tpb = tm // HALO
    gqkv, zo, mqkv, rg = pl.pallas_call(
        functools.partial(_prep_kernel, tm=tm),
        grid=(bsz, nt),
        in_specs=[
            pl.BlockSpec((1, tm, D_MODEL), lambda b, i: (b, i, 0)),
            pl.BlockSpec((1, HALO, D_MODEL), lambda b, i: (b, jnp.maximum(i * tpb - 1, 0), 0)),
            pl.BlockSpec((1, HALO, D_MODEL), lambda b, i: (b, jnp.minimum((i + 1) * tpb, t // HALO - 1), 0)),
            pl.BlockSpec((HALO, D_MODEL), lambda b, i: (N_META // HALO - 1, 0)),
        ] + weight_specs,
        out_specs=[
            pl.BlockSpec((1, tm, QKV_W), lambda b, i: (b, i, 0)),
            pl.BlockSpec((1, tm, ZO_W), lambda b, i: (b, i, 0)),
            pl.BlockSpec((1, tm, MQKV_W), lambda b, i: (b, i, 0)),
            pl.BlockSpec((1, tm // CHUNK, RG_ROWS, LANES), lambda b, i: (b, i, 0, 0)),
        ],
        out_shape=[
            jax.ShapeDtypeStruct((bsz, t, QKV_W), BF16),
            jax.ShapeDtypeStruct((bsz, t, ZO_W), BF16),
            jax.ShapeDtypeStruct((bsz, t, MQKV_W), BF16),
            jax.ShapeDtypeStruct((bsz, nc, RG_ROWS, LANES), F32),
        ],
        compiler_params=cparams(dimension_semantics=("arbitrary", "arbitrary")),
        name="token_prep",
    )(x, x, x, p["meta"], *weights)

    fwd = lambda b, j: (b, j, 0)
    bwd = lambda b, j: (b, nj - 1 - j, 0)
    fwd4 = lambda b, j: (b, j, 0, 0)
    bwd4 = lambda b, j: (b, nj - 1 - j, 0, 0)
    o_f, o_b = pl.pallas_call(
        functools.partial(_scan_kernel, cb=cb),
        grid=(bsz, nj),
        in_specs=[
            pl.BlockSpec((1, tb, QKV_W), fwd),
            pl.BlockSpec((1, tb, MQKV_W), fwd),
            pl.BlockSpec((1, cb, RG_ROWS, LANES), fwd4),
            pl.BlockSpec((1, tb, QKV_W), bwd),
            pl.BlockSpec((1, tb, MQKV_W), bwd),
            pl.BlockSpec((1, cb, RG_ROWS, LANES), bwd4),
        ] + state_specs,
        out_specs=[
            pl.BlockSpec((1, tb, ZO_W), fwd),
            pl.BlockSpec((1, tb, ZO_W), bwd),
        ],
        out_shape=[
            jax.ShapeDtypeStruct((bsz, t, ZO_W), BF16),
            jax.ShapeDtypeStruct((bsz, t, ZO_W), BF16),
        ],
        scratch_shapes=[
            pltpu.VMEM((2,) + s_shape, F32),
            pltpu.VMEM((2,) + c_shape, F32),
            pltpu.VMEM((2,) + m_shape, F32),
        ],
        compiler_params=cparams(dimension_semantics=("arbitrary", "arbitrary")),
        name="chunk_scan",
    )(gqkv, mqkv, rg, gqkv, mqkv, rg, s0, c0, m0)

    rows = bsz * t
    tr = min(512, rows)
    row_spec = lambda w: pl.BlockSpec((tr, w), lambda i: (i, 0))
    y = pl.pallas_call(
        functools.partial(_ffn_kernel, ff_chunk=FF_CHUNK),
        grid=(rows // tr,),
        in_specs=[
            row_spec(D_MODEL), row_spec(ZO_W), row_spec(ZO_W), row_spec(ZO_W),
            _const_spec((1, ZO_W)),
            _const_spec((ZO_W, D_MODEL)),
            _const_spec((1, D_MODEL)),
            _const_spec((1, D_MODEL)),
            _const_spec((D_MODEL, D_FF)),
            _const_spec((D_MODEL, D_FF)),
            _const_spec((D_FF, D_MODEL)),
            _const_spec((1, D_MODEL)),
        ],
        out_specs=row_spec(D_MODEL),
        out_shape=jax.ShapeDtypeStruct((rows, D_MODEL), F32),
        compiler_params=cparams(dimension_semantics=("arbitrary",)),
        name="mix_ffn",
    )(x.reshape(rows, D_MODEL), o_f.reshape(rows, ZO_W), o_b.reshape(rows, ZO_W), zo.reshape(rows, ZO_W),
      p["head_norm"], p["w_out"], p["norm_post_mix"], p["norm_pre_ffn"],
      p["w_gate"], p["w_up"], p["w_down"], p["norm_post_ffn"])
    return y.reshape(bsz, t, D_MODEL)


def _pair_rows(v):
    return jnp.repeat(v.reshape(2 * N_PAIRS, 2), CHUNK, axis=1)


def _prepare_params(meta_tokens, norm_pre_mix, w_in, conv_w, A_log, dt_bias, gdn_norm, i_bias, f_bias,
                    mlstm_norm, w_out, norm_post_mix, norm_pre_ffn, w_gate, w_up, w_down, norm_post_ffn):
    w = w_in[0]
    o_qkv = 0
    o_z = o_qkv + QKV_W
    o_a = o_z + GDN_HEADS * GDN_DV
    o_b = o_a + 2 * GDN_HEADS
    o_mq = o_b + 2 * GDN_HEADS
    o_o = o_mq + MQKV_W
    o_i = o_o + MLSTM_HEADS * MLSTM_DV
    o_f = o_i + 2 * MLSTM_HEADS
    w1 = jnp.concatenate([w[:, o_qkv:o_z], w[:, o_z:o_a], w[:, o_o:o_i], w[:, o_mq:o_o]], axis=1).astype(BF16)
    gcols = jnp.stack([w[:, o:o + 2 * GDN_HEADS] for o in (o_a, o_b, o_i, o_f)], axis=1)
    gcols = gcols.reshape(D_MODEL, 4, 2, N_PAIRS, 2)
    wg = jnp.transpose(gcols, (4, 1, 2, 3, 0)).reshape(N_GATES, D_MODEL).astype(BF16)
    zeros4 = jnp.zeros((2 * N_PAIRS, LANES), F32)
    gadd = jnp.concatenate([_pair_rows(dt_bias[0]), zeros4, _pair_rows(i_bias[0]), _pair_rows(f_bias[0])], axis=0)
    galog = jnp.concatenate([_pair_rows(A_log[0]), zeros4, zeros4, zeros4], axis=0)
    cw = jnp.zeros((SUBLANES, QKV_W), F32).at[0:CONV_K].set(conv_w[0])
    head_norm = jnp.concatenate([jnp.tile(gdn_norm[0], GDN_HEADS), mlstm_norm[0]]).reshape(1, ZO_W)
    return {
        "meta": meta_tokens,
        "norm_pre_mix": norm_pre_mix[0].reshape(1, D_MODEL),
        "w1": w1,
        "wg": wg,
        "conv_w": cw,
        "gadd": gadd,
        "galog": galog,
        "head_norm": head_norm,
        "w_out": w_out[0].astype(BF16),
        "norm_post_mix": norm_post_mix[0].reshape(1, D_MODEL),
        "norm_pre_ffn": norm_pre_ffn[0].reshape(1, D_MODEL),
        "w_gate": w_gate[0].astype(BF16),
        "w_up": w_up[0].astype(BF16),
        "w_down": w_down[0].astype(BF16),
        "norm_post_ffn": norm_post_ffn[0].reshape(1, D_MODEL),
    }


def kernel(x_prompt, x_sample, meta_tokens, norm_pre_mix, w_in, conv_w, A_log, dt_bias, gdn_norm, i_bias, f_bias,
           mlstm_norm, w_out, norm_post_mix, norm_pre_ffn, w_gate, w_up, w_down, norm_post_ffn):
    p = _prepare_params(meta_tokens, norm_pre_mix, w_in, conv_w, A_log, dt_bias, gdn_norm, i_bias, f_bias,
                        mlstm_norm, w_out, norm_post_mix, norm_pre_ffn, w_gate, w_up, w_down, norm_post_ffn)
    return (_encode(x_prompt, p), _encode(x_sample, p))
```

```python
import functools

import jax
import jax.numpy as jnp
from jax import lax
from jax.experimental import pallas as pl
from jax.experimental.pallas import tpu as pltpu

D_MODEL = 1024
N_META = 16
CHUNK = 64
CONV_K = 5
EPS = 1e-6
GDN_HEADS = 4
GDN_DK = 128
GDN_DV = 128
MLSTM_HEADS = 4
MLSTM_DQK = 64
MLSTM_DV = 128
D_FF = 2816
N_PAIRS = 2
QKV_W = 3 * GDN_HEADS * GDN_DK
ZO_W = 2 * GDN_HEADS * GDN_DV
MQKV_W = 2 * MLSTM_HEADS * MLSTM_DQK + MLSTM_HEADS * MLSTM_DV
W1_COLS = QKV_W + ZO_W + MQKV_W
N_GATES = 4 * 2 * GDN_HEADS
LANES = 128
SUBLANES = 8
HALO = SUBLANES
CONV_ROWS = 128
META_ROWS = 2 * CHUNK
FFN_SUBTILES = 4
PREP_SUBTILES = 4
PREP_SUB_ROWS = 256
SCAN_SKEW = 1
SCAN_CHUNKS = 16
FFN_ROWS = 512
MXU_WIDTH = 256
FF_CHUNK = 6 * MXU_WIDTH
NEG_BIG = -1e30
VMEM_LIMIT = 56 * 1024 * 1024

R_GC, R_BETA, R_EG, R_EDB, R_R, R_CM, R_ET0, R_ET1, R_CML, R_BL, R_B = 0, 4, 8, 12, 16, 20, 24, 28, 32, 36, 44
RG_ROWS = 48

F32 = jnp.float32
BF16 = jnp.bfloat16


def _dot(a, b):
    return jnp.dot(a, b, preferred_element_type=F32)


def _dot_nt(a, b):
    return lax.dot_general(a, b, (((1,), (1,)), ((), ())), preferred_element_type=F32)


def _rms(v, w):
    ms = jnp.mean(v * v, axis=-1, keepdims=True)
    return v * lax.rsqrt(ms + EPS) * w


def _softplus(v):
    return jnp.maximum(v, 0.0) + jnp.log(1.0 + jnp.exp(-jnp.abs(v)))


def _lane_scan(cur, op, forward, lanemod):
    n = cur.shape[1]
    k = 1
    while k < CHUNK:
        if forward:
            shifted = pltpu.roll(cur, k, 1)
            ok = lanemod >= k
        else:
            shifted = pltpu.roll(cur, n - k, 1)
            ok = lanemod < CHUNK - k
        cur = jnp.where(ok, op(cur, shifted), cur)
        k *= 2
    return cur


def _pair_layout(a, b):
    tm = a.shape[1]
    low = (lax.broadcasted_iota(jnp.int32, (SUBLANES, tm), 1) & (LANES - 1)) < CHUNK
    even = jnp.where(low, a, pltpu.roll(b, CHUNK, 1))
    odd = jnp.where(low, pltpu.roll(a, tm - CHUNK, 1), b)
    cols = []
    for v in range(tm // LANES):
        cols += [even[:, v * LANES:(v + 1) * LANES], odd[:, v * LANES:(v + 1) * LANES]]
    return jnp.concatenate(cols, axis=1)


def _gate_tiles(gp, gadd_ref, galog_ref, tm, first_valid):
    n = 2 * tm
    top = _pair_layout(gp[0:8], gp[16:24])
    bot = _pair_layout(gp[8:16], gp[24:32])
    row = lax.broadcasted_iota(jnp.int32, (SUBLANES, n), 0)
    lane = lax.broadcasted_iota(jnp.int32, (SUBLANES, n), 1)
    first4 = row < 4
    backward = (row & 2) != 0
    low = (lane & (LANES - 1)) < CHUNK
    lanemod = lane & (CHUNK - 1)
    reps = n // LANES
    gadd = gadd_ref[...]
    ytop = top + jnp.concatenate([gadd[0:8]] * reps, axis=1)
    ybot = bot + jnp.concatenate([gadd[8:16]] * reps, axis=1)
    neg_a = -jnp.exp(jnp.concatenate([galog_ref[0:8, :]] * reps, axis=1))
    etop = jnp.where(first4, neg_a * _softplus(ytop), jax.nn.sigmoid(ytop))
    ebot = jnp.where(first4, ybot, -_softplus(-ybot))
    if first_valid is not None:
        token = lax.shift_right_logical(lane, LANES.bit_length() - 1) * CHUNK + lanemod
        ok = token >= first_valid
        etop = jnp.where(ok, etop, 0.0)
        ebot = jnp.where(ok, ebot, jnp.where(first4, NEG_BIG, 0.0))
    add = lambda a, b: a + b
    swap = lambda a: pltpu.roll(a, 4, 0)

    pre = _lane_scan(etop, add, True, lanemod)
    suf = _lane_scan(etop, add, False, lanemod)
    gc = jnp.where(backward, suf, pre)
    tot = pre + suf - etop
    eg = jnp.exp(gc)
    edb = jnp.exp(tot - gc) * swap(etop)
    etot = jnp.exp(tot)
    et0 = jnp.where(low, etot, pltpu.roll(etot, CHUNK, 1))
    et1 = jnp.where(low, pltpu.roll(etot, n - CHUNK, 1), etot)
    y1 = jnp.where(first4, gc, etop)
    y2 = jnp.where(first4, eg, swap(edb))
    y4 = jnp.where(first4, et0, swap(et1))

    pre = _lane_scan(ebot, add, True, lanemod)
    suf = _lane_scan(ebot, add, False, lanemod)
    bsum = jnp.where(backward, suf, pre)
    btot = pre + suf - ebot
    r = ebot - swap(bsum)
    pmax = _lane_scan(r, jnp.maximum, True, lanemod)
    smax = _lane_scan(r, jnp.maximum, False, lanemod)
    cm = jnp.where(backward, smax, pmax)
    y3 = jnp.where(first4, r, swap(cm))
    y5 = jnp.where(first4, jnp.maximum(pmax, smax), btot)
    return [y1, y2, y3, y4, y5, bsum]


def _prep_rows(hext, hb, w1_ref, wg_ref, cw_ref, gadd_ref, galog_ref, gq_out, tm, first_valid):
    gp = _dot_nt(wg_ref[...], hb)
    ext = _dot(hext, w1_ref[:, 0:QKV_W])
    c0 = QKV_W + ZO_W
    pz = _dot(hb, w1_ref[:, QKV_W:QKV_W + ZO_W // 2])
    po = _dot(hb, w1_ref[:, QKV_W + ZO_W // 2:QKV_W + ZO_W])
    pm = _dot(hb, w1_ref[:, c0:c0 + MQKV_W])
    yield
    tiles = _gate_tiles(gp, gadd_ref, galog_ref, tm, first_valid)
    nrow = tm + 2 * HALO
    taps = [pltpu.roll(ext, ((CONV_K - 1) // 2 - j) % nrow, 0) if j != (CONV_K - 1) // 2 else ext
            for j in range(CONV_K)]

    conv_rows = min(CONV_ROWS, tm)
    for r0 in range(0, tm, conv_rows):
        acc = None
        for j in range(CONV_K):
            term = cw_ref[j:j + 1, :] * taps[j][HALO + r0:HALO + r0 + conv_rows]
            acc = term if acc is None else acc + term
        qkv = acc * jax.nn.sigmoid(acc)
        if first_valid is not None:
            rows = r0 + lax.broadcasted_iota(jnp.int32, (conv_rows, 1), 0)
            qkv = jnp.where(rows >= first_valid, qkv, 0.0)
        parts = []
        for idx in range(2 * GDN_HEADS):
            xh = qkv[:, idx * GDN_DK:(idx + 1) * GDN_DK]
            scale = lax.rsqrt(jnp.sum(xh * xh, axis=-1, keepdims=True) + EPS)
            if idx < GDN_HEADS:
                scale = scale * (GDN_DK ** -0.5)
            parts.append((xh * scale).astype(BF16))
        parts.append(qkv[:, 2 * GDN_HEADS * GDN_DK:].astype(BF16))
        gq_out[pl.ds(r0, conv_rows), :] = jnp.concatenate(parts, axis=1)

    yield
    zo = jnp.concatenate([(pz * jax.nn.sigmoid(pz)).astype(BF16),
                          jax.nn.sigmoid(po).astype(BF16)], axis=1)
    nq = MLSTM_HEADS * MLSTM_DQK
    mqkv = jnp.concatenate([(pm[:, 0:nq] * (MLSTM_DQK ** -0.5)).astype(BF16),
                            pm[:, nq:].astype(BF16)], axis=1)
    return zo, mqkv, tiles


def _drive(gens):
    results = [None] * len(gens)
    live = list(range(len(gens)))
    while live:
        for i in list(live):
            try:
                next(gens[i])
            except StopIteration as stop:
                results[i] = stop.value
                live.remove(i)
    return results


class _Chain:
    pass


def _bd(x, half):
    lane = lax.broadcasted_iota(jnp.int32, x.shape, 1)
    zero = jnp.zeros_like(x)
    return jnp.concatenate([jnp.where(lane < half, x, zero), jnp.where(lane >= half, x, zero)], axis=0)


def _rows_to_cols(rows):
    pad = jnp.zeros((SUBLANES - len(rows), LANES), F32)
    return jnp.concatenate(list(rows) + [pad], axis=0).T


def _pair_cols(cols, j, low):
    first = jnp.broadcast_to(cols[:CHUNK, j:j + 1], (CHUNK, LANES))
    second = jnp.broadcast_to(cols[CHUNK:, j:j + 1], (CHUNK, LANES))
    return jnp.where(low, first, second)


def _chunk_steps(inputs, dirs, states, want_out):
    ti = lax.broadcasted_iota(jnp.int32, (CHUNK, LANES), 0)
    si = lax.broadcasted_iota(jnp.int32, (CHUNK, LANES), 1) & (CHUNK - 1)
    eye2 = jnp.where(ti == si, 1.0, 0.0)
    low = lax.broadcasted_iota(jnp.int32, (CHUNK, LANES), 1) < CHUNK
    low_row = lax.broadcasted_iota(jnp.int32, (1, LANES), 1) < CHUNK
    ones_v = jnp.ones((CHUNK, MLSTM_DV), BF16)
    nq = MLSTM_HEADS * MLSTM_DQK

    steps = []
    m_final = []
    for d, chunks, (_, _, m_in) in zip(dirs, inputs, states):
        incl = ti >= si if d == 0 else ti <= si
        strict = ti > si if d == 0 else ti < si
        m_run = list(m_in)
        per_chunk = []
        for gq, mq, rg in chunks:
            g_step, m_step = [], []
            for p in range(N_PAIRS):
                row = lambda base: rg[base + 2 * d + p:base + 2 * d + p + 1, :]
                c = _Chain()
                c.incl, c.strict = incl, strict
                c.q = gq[:, 2 * GDN_DK * p:2 * GDN_DK * (p + 1)]
                c.k = gq[:, GDN_HEADS * GDN_DK + 2 * GDN_DK * p:GDN_HEADS * GDN_DK + 2 * GDN_DK * (p + 1)]
                c.v = gq[:, 2 * GDN_HEADS * GDN_DK + 2 * GDN_DV * p:2 * GDN_HEADS * GDN_DK + 2 * GDN_DV * (p + 1)]
                c.gc, c.beta, c.eg, c.edb = row(R_GC), row(R_BETA), row(R_EG), row(R_EDB)
                c.etot = jnp.concatenate([row(R_ET0), row(R_ET1)], axis=1)
                g_step.append(c)
                c = _Chain()
                c.incl = incl
                c.q = mq[:, 2 * MLSTM_DQK * p:2 * MLSTM_DQK * (p + 1)]
                c.k = mq[:, nq + 2 * MLSTM_DQK * p:nq + 2 * MLSTM_DQK * (p + 1)]
                v = mq[:, 2 * nq + 2 * MLSTM_DV * p:2 * nq + 2 * MLSTM_DV * (p + 1)]
                c.va = jnp.concatenate([v[:, :MLSTM_DV], ones_v, v[:, MLSTM_DV:], ones_v], axis=1)
                c.m = m_run[p]
                c.r, c.b = row(R_R), row(R_B)
                c.mt = jnp.maximum(c.m, row(R_CM))
                c.ml = jnp.maximum(c.m, row(R_CML))
                m_run[p] = row(R_BL) + c.ml
                m_step.append(c)
            per_chunk.append((g_step, m_step))
        steps.append(per_chunk)
        m_final.append(m_run)

    def stage_qk(gd, ml):
        for c, cm in zip(gd, ml):
            cols = _rows_to_cols([c.gc, cm.mt, -(cm.b + cm.mt)])
            c.gccol = _pair_cols(cols, 0, low)
            cm.mtcol = _pair_cols(cols, 1, low)
            if want_out:
                cm.emt = [jnp.exp(jnp.broadcast_to(cols[:CHUNK, 2:3], (CHUNK, LANES))),
                          jnp.exp(jnp.broadcast_to(cols[CHUNK:, 2:3], (CHUNK, LANES)))]
        for c in gd:
            c.kmask = _bd(c.k, GDN_DK)
            kst = jnp.concatenate([c.k[:, :GDN_DK], c.k[:, GDN_DK:]], axis=0)
            c.kt = kst.T
            kq = _dot(jnp.concatenate([c.q, c.k], axis=0), _bd(c.kt, CHUNK))
            c.qk, c.kk = kq[:CHUNK], kq[CHUNK:]
        for c in ml:
            c.kt = c.k.T
            kt2 = jnp.concatenate([c.kt, c.kt], axis=1)
            rowh = lax.broadcasted_iota(jnp.int32, (2 * MLSTM_DQK, LANES), 0) < MLSTM_DQK
            laneh = lax.broadcasted_iota(jnp.int32, (2 * MLSTM_DQK, LANES), 1) < CHUNK
            c.qk = _dot(c.q, jnp.where(rowh == laneh, kt2, jnp.zeros_like(kt2)))

    def stage_a(gd, ml):
        for c in gd:
            decay = jnp.where(c.incl, jnp.exp(jnp.where(c.incl, c.gccol - c.gc, 0.0)), 0.0)
            a = jnp.where(c.strict, c.kk * decay, 0.0) * c.beta
            c.aqkb = c.qk * decay * c.beta
            c.p = eye2 - a
            ab = a.astype(BF16)
            c.b = _dot(ab, _bd(ab, CHUNK))
        for c in ml:
            w_row = jnp.exp(c.r - c.ml)
            dec = jnp.exp(c.m - c.ml)
            d0 = jnp.where(low_row, dec, pltpu.roll(dec, CHUNK, 1))
            d1 = jnp.where(low_row, pltpu.roll(dec, CHUNK, 1), dec)
            w1 = pltpu.roll(w_row, CHUNK, 1)
            wmat = jnp.concatenate([jnp.broadcast_to(w_row[:, :CHUNK], (MLSTM_DQK, CHUNK)),
                                    jnp.broadcast_to(w1[:, :CHUNK], (MLSTM_DQK, CHUNK))], axis=0)
            ktw = (c.kt.astype(F32) * wmat).astype(BF16)
            c.upd = jnp.concatenate([_dot(ktw[:MLSTM_DQK], c.va[:, :2 * MLSTM_DV]),
                                     _dot(ktw[MLSTM_DQK:], c.va[:, 2 * MLSTM_DV:])], axis=1)
            c.dec = jnp.concatenate([d0, d0, d1, d1], axis=1)

    def stage_power(gd, ml):
        for c in gd:
            bb = c.b.astype(BF16)
            x = _dot(jnp.concatenate([c.p.astype(BF16), bb], axis=0), _bd(bb, CHUNK))
            c.p = c.p + x[:CHUNK]
            c.b = x[CHUNK:]

    def stage_inverse(gd, ml):
        if want_out:
            for c in ml:
                dm = jnp.where(c.incl, jnp.exp(jnp.where(c.incl, c.r - c.mtcol, 0.0)), 0.0)
                c.lhs_t = jnp.concatenate([c.qk * dm, eye2 * jnp.exp(c.m - c.mt)], axis=1).astype(BF16)
                c.va_bd = _bd(c.va, 2 * MLSTM_DV)
        for c in gd:
            c.t = c.p + _dot(c.p.astype(BF16), _bd(c.b.astype(BF16), CHUNK))

    def stage_uw(gd, ml):
        for c in gd:
            c.ut = _dot(c.t.astype(BF16), _bd(c.v, GDN_DV))
            wt = _dot((c.t * c.eg).astype(BF16), c.kmask)
            c.lhs_s = jnp.concatenate([wt.astype(BF16), c.q], axis=0)
            c.kdt = (c.kt.astype(F32) * c.edb).astype(BF16)
            if want_out:
                c.lhs_o = jnp.concatenate([c.aqkb, eye2 * c.eg], axis=1).astype(BF16)

    n_levels = 0
    n = 2
    while 2 * n < CHUNK:
        n_levels += 1
        n *= 2
    stages = [stage_qk, stage_a] + [stage_power] * n_levels + [stage_inverse, stage_uw]

    s_run = [list(st[0]) for st in states]
    c_run = [list(st[1]) for st in states]
    outs = [[] for _ in dirs]

    def state_step(ci):
        now = [(i, p, steps[i][ci][0][p], steps[i][ci][1][p]) for i in range(len(dirs)) for p in range(N_PAIRS)]
        for i, p, g, m in now:
            g.wq = _dot(g.lhs_s, _bd(s_run[i][p].astype(BF16), GDN_DV))
        if want_out:
            for i, p, g, m in now:
                m.qc = _dot(m.q, _bd(c_run[i][p].astype(BF16), 2 * MLSTM_DV))
        for i, p, g, m in now:
            vbd = _bd((g.ut - g.wq[:CHUNK]).astype(BF16), GDN_DV)
            s_run[i][p] = s_run[i][p] * g.etot + _dot(g.kdt, vbd)
            if want_out:
                rhs = jnp.concatenate([vbd, _bd(g.wq[CHUNK:].astype(BF16), GDN_DV)], axis=0)
                g.o = _dot(g.lhs_o, rhs)
        for i, p, g, m in now:
            if want_out:
                rhs = jnp.concatenate([m.va_bd, _bd(m.qc.astype(BF16), 2 * MLSTM_DV)], axis=0)
                m.tot = _dot(m.lhs_t, rhs)
            c_run[i][p] = c_run[i][p] * m.dec + m.upd
        if want_out:
            for i in range(len(dirs)):
                pieces = [g.o for _, _, g, _ in now[i * N_PAIRS:(i + 1) * N_PAIRS]]
                for _, _, _, m in now[i * N_PAIRS:(i + 1) * N_PAIRS]:
                    for h in range(2):
                        num = m.tot[:, 2 * MLSTM_DV * h:2 * MLSTM_DV * h + MLSTM_DV]
                        den = m.tot[:, 2 * MLSTM_DV * h + MLSTM_DV:2 * MLSTM_DV * (h + 1)]
                        pieces.append(num / jnp.maximum(jnp.abs(den), m.emt[h]))
                outs[i].append(pieces)

    n_chunks = len(inputs[0])
    for step in range(SCAN_SKEW * (n_chunks - 1) + len(stages) + 1):
        for ci in range(n_chunks):
            s = step - SCAN_SKEW * ci
            if 0 <= s < len(stages):
                stages[s]([g for st in steps for g in st[ci][0]], [m for st in steps for m in st[ci][1]])
            elif s == len(stages):
                state_step(ci)
    return [(s_run[i], c_run[i], m_final[i]) for i in range(len(dirs))], outs


def _load_state(s_ref, c_ref, m_ref):
    return ([s_ref[p] for p in range(N_PAIRS)], [c_ref[p] for p in range(N_PAIRS)],
            [m_ref[p][0:1, :] for p in range(N_PAIRS)])


def _store_state(state, s_ref, c_ref, m_ref):
    s_out, c_out, m_out = state
    for p in range(N_PAIRS):
        s_ref[p] = s_out[p]
        c_ref[p] = c_out[p]
        m_ref[p] = jnp.broadcast_to(m_out[p], (SUBLANES, LANES))


def _prep_kernel(x_ref, xl_ref, xr_ref, meta_ref, nw_ref, w1_ref, wg_ref, cw_ref, gadd_ref, galog_ref,
                 gqkv_ref, zo_ref, mqkv_ref, rg_ref, *, tm):
    i = pl.program_id(1)
    last = pl.num_programs(1) - 1
    nw = nw_ref[...]
    left = jnp.where(i == 0, meta_ref[...], xl_ref[0])
    right = xr_ref[0] * (i < last).astype(F32)
    hmain = _rms(x_ref[0], nw)
    hext = jnp.concatenate([_rms(left, nw), hmain, _rms(right, nw)], axis=0).astype(BF16)
    hb = hmain.astype(BF16)
    sub = tm // PREP_SUBTILES
    gens = [_prep_rows(hext[i * sub:(i + 1) * sub + 2 * HALO], hb[i * sub:(i + 1) * sub], w1_ref, wg_ref, cw_ref,
                       gadd_ref, galog_ref, gqkv_ref.at[0, i * sub:(i + 1) * sub], sub, None)
            for i in range(PREP_SUBTILES)]
    for i, (zo, mqkv, tiles) in enumerate(_drive(gens)):
        zo_ref[0, i * sub:(i + 1) * sub, :] = zo
        mqkv_ref[0, i * sub:(i + 1) * sub, :] = mqkv
        for c in range(sub // CHUNK):
            for k, y in enumerate(tiles):
                rg_ref[0, i * (sub // CHUNK) + c, k * SUBLANES:(k + 1) * SUBLANES, :] = y[:, c * LANES:(c + 1) * LANES]


def _meta_kernel(xh_ref, meta_ref, nw_ref, w1_ref, wg_ref, cw_ref, gadd_ref, galog_ref,
                 s0_ref, c0_ref, m0_ref, gq_ref):
    nw = nw_ref[...]
    npad = META_ROWS - N_META
    hmain = jnp.concatenate([jnp.zeros((npad, D_MODEL), F32), _rms(meta_ref[...], nw)], axis=0)
    hext = jnp.concatenate([jnp.zeros((HALO, D_MODEL), F32), hmain, _rms(xh_ref[0], nw)], axis=0).astype(BF16)
    ((_, mqkv, tiles),) = _drive([_prep_rows(hext, hmain.astype(BF16), w1_ref, wg_ref, cw_ref, gadd_ref, galog_ref,
                                             gq_ref, META_ROWS, npad)])
    last = META_ROWS // CHUNK - 1
    rg = jnp.concatenate([y[:, last * LANES:(last + 1) * LANES] for y in tiles], axis=0)
    state = ([jnp.zeros((GDN_DK, 2 * GDN_DV), F32)] * N_PAIRS,
             [jnp.zeros((MLSTM_DQK, 4 * MLSTM_DV), F32)] * N_PAIRS,
             [jnp.zeros((1, LANES), F32)] * N_PAIRS)
    chunk = (gq_ref[pl.ds(last * CHUNK, CHUNK), :], mqkv[last * CHUNK:], rg)
    (new_state,), _ = _chunk_steps([[chunk]], (0,), [state], False)
    _store_state(new_state, s0_ref.at[0], c0_ref.at[0], m0_ref.at[0])


def _scan_kernel(gqf_ref, mqf_ref, rgf_ref, gqb_ref, mqb_ref, rgb_ref, s0_ref, c0_ref, m0_ref,
                 of_ref, ob_ref, s_ref, c_ref, m_ref, *, cb):
    j = pl.program_id(1)

    @pl.when(j == 0)
    def _():
        s_ref[0] = s0_ref[0]
        c_ref[0] = c0_ref[0]
        m_ref[0] = m0_ref[0]
        s_ref[1] = jnp.zeros(s_ref.shape[1:], F32)
        c_ref[1] = jnp.zeros(c_ref.shape[1:], F32)
        m_ref[1] = jnp.zeros(m_ref.shape[1:], F32)

    chunk = lambda gq, mq, rg, c: (gq[0, c * CHUNK:(c + 1) * CHUNK, :], mq[0, c * CHUNK:(c + 1) * CHUNK, :], rg[0, c])
    order_b = list(range(cb - 1, -1, -1))
    in_f = [chunk(gqf_ref, mqf_ref, rgf_ref, c) for c in range(cb)]
    in_b = [chunk(gqb_ref, mqb_ref, rgb_ref, c) for c in order_b]
    st_f = (s_ref.at[0], c_ref.at[0], m_ref.at[0])
    st_b = (s_ref.at[1], c_ref.at[1], m_ref.at[1])
    (new_f, new_b), (out_f, out_b) = _chunk_steps([in_f, in_b], (0, 1), [_load_state(*st_f), _load_state(*st_b)], True)
    _store_state(new_f, *st_f)
    _store_state(new_b, *st_b)
    for c in range(cb):
        of_ref[0, c * CHUNK:(c + 1) * CHUNK, :] = jnp.concatenate(out_f[c], axis=1).astype(of_ref.dtype)
        cr = order_b[c]
        ob_ref[0, cr * CHUNK:(cr + 1) * CHUNK, :] = jnp.concatenate(out_b[c], axis=1).astype(ob_ref.dtype)


def _ffn_kernel(x_ref, of_ref, ob_ref, zo_ref, hn_ref, wout_ref, n1_ref, n2_ref, wg_ref, wu_ref, wd_ref, n3_ref,
                y_ref, *, ff_chunk):
    hn = hn_ref[...]
    sub = x_ref.shape[0] // FFN_SUBTILES
    tiles = [_Chain() for _ in range(FFN_SUBTILES)]
    ff_slices = [slice(lo, min(lo + ff_chunk, D_FF)) for lo in range(0, D_FF, ff_chunk)]

    def mix_stage(t, rows):
        o = of_ref[rows, :].astype(F32) + ob_ref[rows, :].astype(F32)
        zo = zo_ref[rows, :].astype(F32)
        parts = []
        for h in range(GDN_HEADS + MLSTM_HEADS):
            sl = slice(h * LANES, (h + 1) * LANES)
            parts.append((_rms(o[:, sl], hn[:, sl]) * zo[:, sl]).astype(BF16))
        t.mix = _dot(jnp.concatenate(parts, axis=1), wout_ref[...])

    def norm_stage(t, rows):
        t.x1 = x_ref[rows, :] + _rms(t.mix, n1_ref[...])
        t.h2 = _rms(t.x1, n2_ref[...]).astype(BF16)
        t.f = None

    def ff_stage(c):
        def run(t, rows):
            gate = _dot(t.h2, wg_ref[:, ff_slices[c]])
            up = _dot(t.h2, wu_ref[:, ff_slices[c]])
            act = (gate * jax.nn.sigmoid(gate) * up).astype(BF16)
            part = _dot(act, wd_ref[ff_slices[c], :])
            t.f = part if t.f is None else t.f + part
        return run

    def out_stage(t, rows):
        y_ref[rows, :] = t.x1 + _rms(t.f, n3_ref[...])

    stages = [mix_stage, norm_stage] + [ff_stage(c) for c in range(len(ff_slices))] + [out_stage]
    for step in range(len(stages) + FFN_SUBTILES - 1):
        for i, t in enumerate(tiles):
            if 0 <= step - i < len(stages):
                stages[step - i](t, slice(i * sub, (i + 1) * sub))


def _const_spec(shape):
    nd = len(shape)
    return pl.BlockSpec(shape, lambda *_: (0,) * nd, pipeline_mode=pl.Buffered(1))


def _encode(x, p):
    bsz, t, _ = x.shape
    tm = min(PREP_SUBTILES * PREP_SUB_ROWS, t)
    nt = t // tm
    cb = min(SCAN_CHUNKS, t // CHUNK)
    tb = cb * CHUNK
    nj = t // tb
    nc = t // CHUNK
    cparams = functools.partial(pltpu.CompilerParams, vmem_limit_bytes=VMEM_LIMIT)
    weight_specs = [
        _const_spec((1, D_MODEL)),
        _const_spec((D_MODEL, W1_COLS)),
        _const_spec((N_GATES, D_MODEL)),
        _const_spec((SUBLANES, QKV_W)),
        _const_spec((2 * SUBLANES, LANES)),
        _const_spec((2 * SUBLANES, LANES)),
    ]
    weights = (p["norm_pre_mix"], p["w1"], p["wg"], p["conv_w"], p["gadd"], p["galog"])
    s_shape = (N_PAIRS, GDN_DK, 2 * GDN_DV)
    c_shape = (N_PAIRS, MLSTM_DQK, 4 * MLSTM_DV)
    m_shape = (N_PAIRS, SUBLANES, LANES)
    state_specs = [pl.BlockSpec((1,) + s, lambda b, *_: (b, 0, 0, 0)) for s in (s_shape, c_shape, m_shape)]

    s0, c0, m0 = pl.pallas_call(
        _meta_kernel,
        grid=(bsz,),
        in_specs=[pl.BlockSpec((1, HALO, D_MODEL), lambda b: (b, 0, 0)),
                  _const_spec((N_META, D_MODEL))] + weight_specs,
        out_specs=state_specs,
        out_shape=[jax.ShapeDtypeStruct((bsz,) + s, F32) for s in (s_shape, c_shape, m_shape)],
        scratch_shapes=[
            pltpu.VMEM((META_ROWS, QKV_W), BF16),
        ],
        compiler_params=cparams(dimension_semantics=("arbitrary",)),
        name="meta_state",
    )(x, p["meta"], *weights)

    tpb = tm // HALO
    gqkv, zo, mqkv, rg = pl.pallas_call(
        functools.partial(_prep_kernel, tm=tm),
        grid=(bsz, nt),
        in_specs=[
            pl.BlockSpec((1, tm, D_MODEL), lambda b, i: (b, i, 0)),
            pl.BlockSpec((1, HALO, D_MODEL), lambda b, i: (b, jnp.maximum(i * tpb - 1, 0), 0)),
            pl.BlockSpec((1, HALO, D_MODEL), lambda b, i: (b, jnp.minimum((i + 1) * tpb, t // HALO - 1), 0)),
            pl.BlockSpec((HALO, D_MODEL), lambda b, i: (N_META // HALO - 1, 0)),
        ] + weight_specs,
        out_specs=[
            pl.BlockSpec((1, tm, QKV_W), lambda b, i: (b, i, 0)),
            pl.BlockSpec((1, tm, ZO_W), lambda b, i: (b, i, 0)),
            pl.BlockSpec((1, tm, MQKV_W), lambda b, i: (b, i, 0)),
            pl.BlockSpec((1, tm // CHUNK, RG_ROWS, LANES), lambda b, i: (b, i, 0, 0)),
        ],
        out_shape=[
            jax.ShapeDtypeStruct((bsz, t, QKV_W), BF16),
            jax.ShapeDtypeStruct((bsz, t, ZO_W), BF16),
            jax.ShapeDtypeStruct((bsz, t, MQKV_W), BF16),
            jax.ShapeDtypeStruct((bsz, nc, RG_ROWS, LANES), F32),
        ],
        compiler_params=cparams(dimension_semantics=("arbitrary", "arbitrary")),
        name="token_prep",
    )(x, x, x, p["meta"], *weights)

    fwd = lambda b, j: (b, j, 0)
    bwd = lambda b, j: (b, nj - 1 - j, 0)
    fwd4 = lambda b, j: (b, j, 0, 0)
    bwd4 = lambda b, j: (b, nj - 1 - j, 0, 0)
    o_f, o_b = pl.pallas_call(
        functools.partial(_scan_kernel, cb=cb),
        grid=(bsz, nj),
        in_specs=[
            pl.BlockSpec((1, tb, QKV_W), fwd),
            pl.BlockSpec((1, tb, MQKV_W), fwd),
            pl.BlockSpec((1, cb, RG_ROWS, LANES), fwd4),
            pl.BlockSpec((1, tb, QKV_W), bwd),
            pl.BlockSpec((1, tb, MQKV_W), bwd),
            pl.BlockSpec((1, cb, RG_ROWS, LANES), bwd4),
        ] + state_specs,
        out_specs=[
            pl.BlockSpec((1, tb, ZO_W), fwd),
            pl.BlockSpec((1, tb, ZO_W), bwd),
        ],
        out_shape=[
            jax.ShapeDtypeStruct((bsz, t, ZO_W), BF16),
            jax.ShapeDtypeStruct((bsz, t, ZO_W), BF16),
        ],
        scratch_shapes=[
            pltpu.VMEM((2,) + s_shape, F32),
            pltpu.VMEM((2,) + c_shape, F32),
            pltpu.VMEM((2,) + m_shape, F32),
        ],
        compiler_params=cparams(dimension_semantics=("arbitrary", "arbitrary")),
        name="chunk_scan",
    )(gqkv, mqkv, rg, gqkv, mqkv, rg, s0, c0, m0)

    rows = bsz * t
    tr = min(FFN_ROWS, rows)
    row_spec = lambda w: pl.BlockSpec((tr, w), lambda i: (i, 0))
    y = pl.pallas_call(
        functools.partial(_ffn_kernel, ff_chunk=FF_CHUNK),
        grid=(rows // tr,),
        in_specs=[
            row_spec(D_MODEL), row_spec(ZO_W), row_spec(ZO_W), row_spec(ZO_W),
            _const_spec((1, ZO_W)),
            _const_spec((ZO_W, D_MODEL)),
            _const_spec((1, D_MODEL)),
            _const_spec((1, D_MODEL)),
            _const_spec((D_MODEL, D_FF)),
            _const_spec((D_MODEL, D_FF)),
            _const_spec((D_FF, D_MODEL)),
            _const_spec((1, D_MODEL)),
        ],
        out_specs=row_spec(D_MODEL),
        out_shape=jax.ShapeDtypeStruct((rows, D_MODEL), F32),
        compiler_params=cparams(dimension_semantics=("arbitrary",)),
        name="mix_ffn",
    )(x.reshape(rows, D_MODEL), o_f.reshape(rows, ZO_W), o_b.reshape(rows, ZO_W), zo.reshape(rows, ZO_W),
      p["head_norm"], p["w_out"], p["norm_post_mix"], p["norm_pre_ffn"],
      p["w_gate"], p["w_up"], p["w_down"], p["norm_post_ffn"])
    return y.reshape(bsz, t, D_MODEL)


def _pair_rows(v):
    return jnp.repeat(v.reshape(2 * N_PAIRS, 2), CHUNK, axis=1)


def _prepare_params(meta_tokens, norm_pre_mix, w_in, conv_w, A_log, dt_bias, gdn_norm, i_bias, f_bias,
                    mlstm_norm, w_out, norm_post_mix, norm_pre_ffn, w_gate, w_up, w_down, norm_post_ffn):
    w = w_in[0]
    o_qkv = 0
    o_z = o_qkv + QKV_W
    o_a = o_z + GDN_HEADS * GDN_DV
    o_b = o_a + 2 * GDN_HEADS
    o_mq = o_b + 2 * GDN_HEADS
    o_o = o_mq + MQKV_W
    o_i = o_o + MLSTM_HEADS * MLSTM_DV
    o_f = o_i + 2 * MLSTM_HEADS
    w1 = jnp.concatenate([w[:, o_qkv:o_z], w[:, o_z:o_a], w[:, o_o:o_i], w[:, o_mq:o_o]], axis=1).astype(BF16)
    gcols = jnp.stack([w[:, o:o + 2 * GDN_HEADS] for o in (o_a, o_b, o_i, o_f)], axis=1)
    gcols = gcols.reshape(D_MODEL, 4, 2, N_PAIRS, 2)
    wg = jnp.transpose(gcols, (4, 1, 2, 3, 0)).reshape(N_GATES, D_MODEL).astype(BF16)
    zeros4 = jnp.zeros((2 * N_PAIRS, LANES), F32)
    gadd = jnp.concatenate([_pair_rows(dt_bias[0]), zeros4, _pair_rows(i_bias[0]), _pair_rows(f_bias[0])], axis=0)
    galog = jnp.concatenate([_pair_rows(A_log[0]), zeros4, zeros4, zeros4], axis=0)
    cw = jnp.zeros((SUBLANES, QKV_W), F32).at[0:CONV_K].set(conv_w[0])
    head_norm = jnp.concatenate([jnp.tile(gdn_norm[0], GDN_HEADS), mlstm_norm[0]]).reshape(1, ZO_W)
    return {
        "meta": meta_tokens,
        "norm_pre_mix": norm_pre_mix[0].reshape(1, D_MODEL),
        "w1": w1,
        "wg": wg,
        "conv_w": cw,
        "gadd": gadd,
        "galog": galog,
        "head_norm": head_norm,
        "w_out": w_out[0].astype(BF16),
        "norm_post_mix": norm_post_mix[0].reshape(1, D_MODEL),
        "norm_pre_ffn": norm_pre_ffn[0].reshape(1, D_MODEL),
        "w_gate": w_gate[0].astype(BF16),
        "w_up": w_up[0].astype(BF16),
        "w_down": w_down[0].astype(BF16),
        "norm_post_ffn": norm_post_ffn[0].reshape(1, D_MODEL),
    }


def kernel(x_prompt, x_sample, meta_tokens, norm_pre_mix, w_in, conv_w, A_log, dt_bias, gdn_norm, i_bias, f_bias,
           mlstm_norm, w_out, norm_post_mix, norm_pre_ffn, w_gate, w_up, w_down, norm_post_ffn):
    p = _prepare_params(meta_tokens, norm_pre_mix, w_in, conv_w, A_log, dt_bias, gdn_norm, i_bias, f_bias,
                        mlstm_norm, w_out, norm_post_mix, norm_pre_ffn, w_gate, w_up, w_down, norm_post_ffn)
    return (_encode(x_prompt, p), _encode(x_sample, p))
```

```python
import functools

import jax
import jax.numpy as jnp
from jax import lax
from jax.experimental import pallas as pl
from jax.experimental.pallas import tpu as pltpu

D_MODEL = 1024
N_META = 16
CHUNK = 64
CONV_K = 5
EPS = 1e-6
GDN_HEADS = 4
GDN_DK = 128
GDN_DV = 128
MLSTM_HEADS = 4
MLSTM_DQK = 64
MLSTM_DV = 128
D_FF = 2816
N_PAIRS = 2
QKV_W = 3 * GDN_HEADS * GDN_DK
ZO_W = 2 * GDN_HEADS * GDN_DV
MQKV_W = 2 * MLSTM_HEADS * MLSTM_DQK + MLSTM_HEADS * MLSTM_DV
W1_COLS = QKV_W + ZO_W + MQKV_W
N_GATES = 4 * 2 * GDN_HEADS
LANES = 128
SUBLANES = 8
HALO = SUBLANES
CONV_ROWS = 128
META_ROWS = 2 * CHUNK
FFN_SUBTILES = 4
PREP_SUBTILES = 4
PREP_SUB_ROWS = 256
MXU_WIDTH = 256
FF_CHUNK = 6 * MXU_WIDTH
NEG_BIG = -1e30
VMEM_LIMIT = 62 * 1024 * 1024

R_GC, R_BETA, R_EG, R_EDB, R_R, R_CM, R_ET0, R_ET1, R_CML, R_BL, R_B = 0, 4, 8, 12, 16, 20, 24, 28, 32, 36, 44
RG_ROWS = 48

F32 = jnp.float32
BF16 = jnp.bfloat16


def _dot(a, b):
    return jnp.dot(a, b, preferred_element_type=F32)


def _dot_nt(a, b):
    return lax.dot_general(a, b, (((1,), (1,)), ((), ())), preferred_element_type=F32)


def _rms(v, w):
    ms = jnp.mean(v * v, axis=-1, keepdims=True)
    return v * lax.rsqrt(ms + EPS) * w


def _softplus(v):
    return jnp.maximum(v, 0.0) + jnp.log(1.0 + jnp.exp(-jnp.abs(v)))


def _lane_scan(cur, op, forward, lanemod):
    n = cur.shape[1]
    k = 1
    while k < CHUNK:
        if forward:
            shifted = pltpu.roll(cur, k, 1)
            ok = lanemod >= k
        else:
            shifted = pltpu.roll(cur, n - k, 1)
            ok = lanemod < CHUNK - k
        cur = jnp.where(ok, op(cur, shifted), cur)
        k *= 2
    return cur


def _pair_layout(a, b):
    tm = a.shape[1]
    low = (lax.broadcasted_iota(jnp.int32, (SUBLANES, tm), 1) & (LANES - 1)) < CHUNK
    even = jnp.where(low, a, pltpu.roll(b, CHUNK, 1))
    odd = jnp.where(low, pltpu.roll(a, tm - CHUNK, 1), b)
    cols = []
    for v in range(tm // LANES):
        cols += [even[:, v * LANES:(v + 1) * LANES], odd[:, v * LANES:(v + 1) * LANES]]
    return jnp.concatenate(cols, axis=1)


def _gate_tiles(gp, gadd_ref, galog_ref, tm, first_valid):
    n = 2 * tm
    top = _pair_layout(gp[0:8], gp[16:24])
    bot = _pair_layout(gp[8:16], gp[24:32])
    row = lax.broadcasted_iota(jnp.int32, (SUBLANES, n), 0)
    lane = lax.broadcasted_iota(jnp.int32, (SUBLANES, n), 1)
    first4 = row < 4
    backward = (row & 2) != 0
    low = (lane & (LANES - 1)) < CHUNK
    lanemod = lane & (CHUNK - 1)
    reps = n // LANES
    gadd = gadd_ref[...]
    ytop = top + jnp.concatenate([gadd[0:8]] * reps, axis=1)
    ybot = bot + jnp.concatenate([gadd[8:16]] * reps, axis=1)
    neg_a = -jnp.exp(jnp.concatenate([galog_ref[0:8, :]] * reps, axis=1))
    etop = jnp.where(first4, neg_a * _softplus(ytop), jax.nn.sigmoid(ytop))
    ebot = jnp.where(first4, ybot, -_softplus(-ybot))
    if first_valid is not None:
        token = lax.shift_right_logical(lane, LANES.bit_length() - 1) * CHUNK + lanemod
        ok = token >= first_valid
        etop = jnp.where(ok, etop, 0.0)
        ebot = jnp.where(ok, ebot, jnp.where(first4, NEG_BIG, 0.0))
    add = lambda a, b: a + b
    swap = lambda a: pltpu.roll(a, 4, 0)

    pre = _lane_scan(etop, add, True, lanemod)
    suf = _lane_scan(etop, add, False, lanemod)
    gc = jnp.where(backward, suf, pre)
    tot = pre + suf - etop
    eg = jnp.exp(gc)
    edb = jnp.exp(tot - gc) * swap(etop)
    etot = jnp.exp(tot)
    et0 = jnp.where(low, etot, pltpu.roll(etot, CHUNK, 1))
    et1 = jnp.where(low, pltpu.roll(etot, n - CHUNK, 1), etot)
    y1 = jnp.where(first4, gc, etop)
    y2 = jnp.where(first4, eg, swap(edb))
    y4 = jnp.where(first4, et0, swap(et1))

    pre = _lane_scan(ebot, add, True, lanemod)
    suf = _lane_scan(ebot, add, False, lanemod)
    bsum = jnp.where(backward, suf, pre)
    btot = pre + suf - ebot
    r = ebot - swap(bsum)
    pmax = _lane_scan(r, jnp.maximum, True, lanemod)
    smax = _lane_scan(r, jnp.maximum, False, lanemod)
    cm = jnp.where(backward, smax, pmax)
    y3 = jnp.where(first4, r, swap(cm))
    y5 = jnp.where(first4, jnp.maximum(pmax, smax), btot)
    return [y1, y2, y3, y4, y5, bsum]


def _prep_rows(hext, hb, w1_ref, wg_ref, cw_ref, gadd_ref, galog_ref, gq_out, tm, first_valid):
    gp = _dot_nt(wg_ref[...], hb)
    ext = _dot(hext, w1_ref[:, 0:QKV_W])
    c0 = QKV_W + ZO_W
    pz = _dot(hb, w1_ref[:, QKV_W:QKV_W + ZO_W // 2])
    po = _dot(hb, w1_ref[:, QKV_W + ZO_W // 2:QKV_W + ZO_W])
    pm = _dot(hb, w1_ref[:, c0:c0 + MQKV_W])
    yield
    tiles = _gate_tiles(gp, gadd_ref, galog_ref, tm, first_valid)
    nrow = tm + 2 * HALO
    taps = [pltpu.roll(ext, ((CONV_K - 1) // 2 - j) % nrow, 0) if j != (CONV_K - 1) // 2 else ext
            for j in range(CONV_K)]

    conv_rows = min(CONV_ROWS, tm)
    for r0 in range(0, tm, conv_rows):
        acc = None
        for j in range(CONV_K):
            term = cw_ref[j:j + 1, :] * taps[j][HALO + r0:HALO + r0 + conv_rows]
            acc = term if acc is None else acc + term
        qkv = acc * jax.nn.sigmoid(acc)
        if first_valid is not None:
            rows = r0 + lax.broadcasted_iota(jnp.int32, (conv_rows, 1), 0)
            qkv = jnp.where(rows >= first_valid, qkv, 0.0)
        parts = []
        for idx in range(2 * GDN_HEADS):
            xh = qkv[:, idx * GDN_DK:(idx + 1) * GDN_DK]
            scale = lax.rsqrt(jnp.sum(xh * xh, axis=-1, keepdims=True) + EPS)
            if idx < GDN_HEADS:
                scale = scale * (GDN_DK ** -0.5)
            parts.append((xh * scale).astype(BF16))
        parts.append(qkv[:, 2 * GDN_HEADS * GDN_DK:].astype(BF16))
        gq_out[pl.ds(r0, conv_rows), :] = jnp.concatenate(parts, axis=1)

    yield
    zo = jnp.concatenate([(pz * jax.nn.sigmoid(pz)).astype(BF16),
                          jax.nn.sigmoid(po).astype(BF16)], axis=1)
    nq = MLSTM_HEADS * MLSTM_DQK
    mqkv = jnp.concatenate([(pm[:, 0:nq] * (MLSTM_DQK ** -0.5)).astype(BF16),
                            pm[:, nq:].astype(BF16)], axis=1)
    return zo, mqkv, tiles


def _drive(gens):
    results = [None] * len(gens)
    live = list(range(len(gens)))
    while live:
        for i in list(live):
            try:
                next(gens[i])
            except StopIteration as stop:
                results[i] = stop.value
                live.remove(i)
    return results


class _Chain:
    pass


def _bd(x, half):
    lane = lax.broadcasted_iota(jnp.int32, x.shape, 1)
    zero = jnp.zeros_like(x)
    return jnp.concatenate([jnp.where(lane < half, x, zero), jnp.where(lane >= half, x, zero)], axis=0)


def _rows_to_cols(rows):
    pad = jnp.zeros((SUBLANES - len(rows), LANES), F32)
    return jnp.concatenate(list(rows) + [pad], axis=0).T


def _pair_cols(cols, j, low):
    first = jnp.broadcast_to(cols[:CHUNK, j:j + 1], (CHUNK, LANES))
    second = jnp.broadcast_to(cols[CHUNK:, j:j + 1], (CHUNK, LANES))
    return jnp.where(low, first, second)


def _chunk_steps(inputs, dirs, states, want_out):
    ti = lax.broadcasted_iota(jnp.int32, (CHUNK, LANES), 0)
    si = lax.broadcasted_iota(jnp.int32, (CHUNK, LANES), 1) & (CHUNK - 1)
    eye2 = jnp.where(ti == si, 1.0, 0.0)
    low = lax.broadcasted_iota(jnp.int32, (CHUNK, LANES), 1) < CHUNK
    low_row = lax.broadcasted_iota(jnp.int32, (1, LANES), 1) < CHUNK
    ones_v = jnp.ones((CHUNK, MLSTM_DV), BF16)
    nq = MLSTM_HEADS * MLSTM_DQK

    gd, ml = [], []
    steps = []
    m_final = []
    for d, chunks, (_, _, m_in) in zip(dirs, inputs, states):
        incl = ti >= si if d == 0 else ti <= si
        strict = ti > si if d == 0 else ti < si
        m_run = list(m_in)
        per_chunk = []
        for gq, mq, rg in chunks:
            g_step, m_step = [], []
            for p in range(N_PAIRS):
                row = lambda base: rg[base + 2 * d + p:base + 2 * d + p + 1, :]
                c = _Chain()
                c.incl, c.strict = incl, strict
                c.q = gq[:, 2 * GDN_DK * p:2 * GDN_DK * (p + 1)]
                c.k = gq[:, GDN_HEADS * GDN_DK + 2 * GDN_DK * p:GDN_HEADS * GDN_DK + 2 * GDN_DK * (p + 1)]
                c.v = gq[:, 2 * GDN_HEADS * GDN_DK + 2 * GDN_DV * p:2 * GDN_HEADS * GDN_DK + 2 * GDN_DV * (p + 1)]
                c.gc, c.beta, c.eg, c.edb = row(R_GC), row(R_BETA), row(R_EG), row(R_EDB)
                c.etot = jnp.concatenate([row(R_ET0), row(R_ET1)], axis=1)
                g_step.append(c)
                c = _Chain()
                c.incl = incl
                c.q = mq[:, 2 * MLSTM_DQK * p:2 * MLSTM_DQK * (p + 1)]
                c.k = mq[:, nq + 2 * MLSTM_DQK * p:nq + 2 * MLSTM_DQK * (p + 1)]
                v = mq[:, 2 * nq + 2 * MLSTM_DV * p:2 * nq + 2 * MLSTM_DV * (p + 1)]
                c.va = jnp.concatenate([v[:, :MLSTM_DV], ones_v, v[:, MLSTM_DV:], ones_v], axis=1)
                c.m = m_run[p]
                c.r, c.b = row(R_R), row(R_B)
                c.mt = jnp.maximum(c.m, row(R_CM))
                c.ml = jnp.maximum(c.m, row(R_CML))
                m_run[p] = row(R_BL) + c.ml
                m_step.append(c)
            per_chunk.append((g_step, m_step))
            gd += g_step
            ml += m_step
        steps.append(per_chunk)
        m_final.append(m_run)

    def stage_qk(gd, ml):
        for c, cm in zip(gd, ml):
            cols = _rows_to_cols([c.gc, cm.mt, -(cm.b + cm.mt)])
            c.gccol = _pair_cols(cols, 0, low)
            cm.mtcol = _pair_cols(cols, 1, low)
            if want_out:
                cm.emt = [jnp.exp(jnp.broadcast_to(cols[:CHUNK, 2:3], (CHUNK, LANES))),
                          jnp.exp(jnp.broadcast_to(cols[CHUNK:, 2:3], (CHUNK, LANES)))]
        for c in gd:
            c.kmask = _bd(c.k, GDN_DK)
            kst = jnp.concatenate([c.k[:, :GDN_DK], c.k[:, GDN_DK:]], axis=0)
            c.kt = kst.T
            kq = _dot(jnp.concatenate([c.q, c.k], axis=0), _bd(c.kt, CHUNK))
            c.qk, c.kk = kq[:CHUNK], kq[CHUNK:]
        for c in ml:
            c.kt = c.k.T
            kt2 = jnp.concatenate([c.kt, c.kt], axis=1)
            rowh = lax.broadcasted_iota(jnp.int32, (2 * MLSTM_DQK, LANES), 0) < MLSTM_DQK
            laneh = lax.broadcasted_iota(jnp.int32, (2 * MLSTM_DQK, LANES), 1) < CHUNK
            c.qk = _dot(c.q, jnp.where(rowh == laneh, kt2, jnp.zeros_like(kt2)))

    def stage_a(gd, ml):
        for c in gd:
            decay = jnp.where(c.incl, jnp.exp(jnp.where(c.incl, c.gccol - c.gc, 0.0)), 0.0)
            a = jnp.where(c.strict, c.kk * decay, 0.0) * c.beta
            c.aqkb = c.qk * decay * c.beta
            c.p = eye2 - a
            ab = a.astype(BF16)
            c.b = _dot(ab, _bd(ab, CHUNK))
        for c in ml:
            w_row = jnp.exp(c.r - c.ml)
            dec = jnp.exp(c.m - c.ml)
            d0 = jnp.where(low_row, dec, pltpu.roll(dec, CHUNK, 1))
            d1 = jnp.where(low_row, pltpu.roll(dec, CHUNK, 1), dec)
            w1 = pltpu.roll(w_row, CHUNK, 1)
            wmat = jnp.concatenate([jnp.broadcast_to(w_row[:, :CHUNK], (MLSTM_DQK, CHUNK)),
                                    jnp.broadcast_to(w1[:, :CHUNK], (MLSTM_DQK, CHUNK))], axis=0)
            ktw = (c.kt.astype(F32) * wmat).astype(BF16)
            c.upd = jnp.concatenate([_dot(ktw[:MLSTM_DQK], c.va[:, :2 * MLSTM_DV]),
                                     _dot(ktw[MLSTM_DQK:], c.va[:, 2 * MLSTM_DV:])], axis=1)
            c.dec = jnp.concatenate([d0, d0, d1, d1], axis=1)

    def stage_power(gd, ml):
        for c in gd:
            bb = c.b.astype(BF16)
            x = _dot(jnp.concatenate([c.p.astype(BF16), bb], axis=0), _bd(bb, CHUNK))
            c.p = c.p + x[:CHUNK]
            c.b = x[CHUNK:]

    def stage_inverse(gd, ml):
        if want_out:
            for c in ml:
                dm = jnp.where(c.incl, jnp.exp(jnp.where(c.incl, c.r - c.mtcol, 0.0)), 0.0)
                c.lhs_t = jnp.concatenate([c.qk * dm, eye2 * jnp.exp(c.m - c.mt)], axis=1).astype(BF16)
                c.va_bd = _bd(c.va, 2 * MLSTM_DV)
        for c in gd:
            c.t = c.p + _dot(c.p.astype(BF16), _bd(c.b.astype(BF16), CHUNK))

    def stage_uw(gd, ml):
        for c in gd:
            c.ut = _dot(c.t.astype(BF16), _bd(c.v, GDN_DV))
            wt = _dot((c.t * c.eg).astype(BF16), c.kmask)
            c.lhs_s = jnp.concatenate([wt.astype(BF16), c.q], axis=0)
            c.kdt = (c.kt.astype(F32) * c.edb).astype(BF16)
            if want_out:
                c.lhs_o = jnp.concatenate([c.aqkb, eye2 * c.eg], axis=1).astype(BF16)

    n_levels = 0
    n = 2
    while 2 * n < CHUNK:
        n_levels += 1
        n *= 2
    stages = [stage_qk, stage_a] + [stage_power] * n_levels + [stage_inverse, stage_uw]

    s_run = [list(st[0]) for st in states]
    c_run = [list(st[1]) for st in states]
    outs = [[] for _ in dirs]

    def state_step(ci):
        now = [(i, p, steps[i][ci][0][p], steps[i][ci][1][p]) for i in range(len(dirs)) for p in range(N_PAIRS)]
        for i, p, g, m in now:
            g.wq = _dot(g.lhs_s, _bd(s_run[i][p].astype(BF16), GDN_DV))
        if want_out:
            for i, p, g, m in now:
                m.qc = _dot(m.q, _bd(c_run[i][p].astype(BF16), 2 * MLSTM_DV))
        for i, p, g, m in now:
            vbd = _bd((g.ut - g.wq[:CHUNK]).astype(BF16), GDN_DV)
            s_run[i][p] = s_run[i][p] * g.etot + _dot(g.kdt, vbd)
            if want_out:
                rhs = jnp.concatenate([vbd, _bd(g.wq[CHUNK:].astype(BF16), GDN_DV)], axis=0)
                g.o = _dot(g.lhs_o, rhs)
        for i, p, g, m in now:
            if want_out:
                rhs = jnp.concatenate([m.va_bd, _bd(m.qc.astype(BF16), 2 * MLSTM_DV)], axis=0)
                m.tot = _dot(m.lhs_t, rhs)
            c_run[i][p] = c_run[i][p] * m.dec + m.upd
        if want_out:
            for i in range(len(dirs)):
                pieces = [g.o for _, _, g, _ in now[i * N_PAIRS:(i + 1) * N_PAIRS]]
                for _, _, _, m in now[i * N_PAIRS:(i + 1) * N_PAIRS]:
                    for h in range(2):
                        num = m.tot[:, 2 * MLSTM_DV * h:2 * MLSTM_DV * h + MLSTM_DV]
                        den = m.tot[:, 2 * MLSTM_DV * h + MLSTM_DV:2 * MLSTM_DV * (h + 1)]
                        pieces.append(num / jnp.maximum(jnp.abs(den), m.emt[h]))
                outs[i].append(pieces)

    n_chunks = len(inputs[0])
    for step in range(n_chunks + len(stages)):
        for ci in range(n_chunks):
            s = step - ci
            if 0 <= s < len(stages):
                stages[s]([g for st in steps for g in st[ci][0]], [m for st in steps for m in st[ci][1]])
            elif s == len(stages):
                state_step(ci)
    return [(s_run[i], c_run[i], m_final[i]) for i in range(len(dirs))], outs


def _load_state(s_ref, c_ref, m_ref):
    return ([s_ref[p] for p in range(N_PAIRS)], [c_ref[p] for p in range(N_PAIRS)],
            [m_ref[p][0:1, :] for p in range(N_PAIRS)])


def _store_state(state, s_ref, c_ref, m_ref):
    s_out, c_out, m_out = state
    for p in range(N_PAIRS):
        s_ref[p] = s_out[p]
        c_ref[p] = c_out[p]
        m_ref[p] = jnp.broadcast_to(m_out[p], (SUBLANES, LANES))


def _prep_kernel(x_ref, xl_ref, xr_ref, meta_ref, nw_ref, w1_ref, wg_ref, cw_ref, gadd_ref, galog_ref,
                 gqkv_ref, zo_ref, mqkv_ref, rg_ref, *, tm):
    i = pl.program_id(1)
    last = pl.num_programs(1) - 1
    nw = nw_ref[...]
    left = jnp.where(i == 0, meta_ref[...], xl_ref[0])
    right = xr_ref[0] * (i < last).astype(F32)
    hmain = _rms(x_ref[0], nw)
    hext = jnp.concatenate([_rms(left, nw), hmain, _rms(right, nw)], axis=0).astype(BF16)
    hb = hmain.astype(BF16)
    sub = tm // PREP_SUBTILES
    gens = [_prep_rows(hext[i * sub:(i + 1) * sub + 2 * HALO], hb[i * sub:(i + 1) * sub], w1_ref, wg_ref, cw_ref,
                       gadd_ref, galog_ref, gqkv_ref.at[0, i * sub:(i + 1) * sub], sub, None)
            for i in range(PREP_SUBTILES)]
    for i, (zo, mqkv, tiles) in enumerate(_drive(gens)):
        zo_ref[0, i * sub:(i + 1) * sub, :] = zo
        mqkv_ref[0, i * sub:(i + 1) * sub, :] = mqkv
        for c in range(sub // CHUNK):
            for k, y in enumerate(tiles):
                rg_ref[0, i * (sub // CHUNK) + c, k * SUBLANES:(k + 1) * SUBLANES, :] = y[:, c * LANES:(c + 1) * LANES]


def _meta_kernel(xh_ref, meta_ref, nw_ref, w1_ref, wg_ref, cw_ref, gadd_ref, galog_ref,
                 s0_ref, c0_ref, m0_ref, gq_ref):
    nw = nw_ref[...]
    npad = META_ROWS - N_META
    hmain = jnp.concatenate([jnp.zeros((npad, D_MODEL), F32), _rms(meta_ref[...], nw)], axis=0)
    hext = jnp.concatenate([jnp.zeros((HALO, D_MODEL), F32), hmain, _rms(xh_ref[0], nw)], axis=0).astype(BF16)
    ((_, mqkv, tiles),) = _drive([_prep_rows(hext, hmain.astype(BF16), w1_ref, wg_ref, cw_ref, gadd_ref, galog_ref,
                                             gq_ref, META_ROWS, npad)])
    last = META_ROWS // CHUNK - 1
    rg = jnp.concatenate([y[:, last * LANES:(last + 1) * LANES] for y in tiles], axis=0)
    state = ([jnp.zeros((GDN_DK, 2 * GDN_DV), F32)] * N_PAIRS,
             [jnp.zeros((MLSTM_DQK, 4 * MLSTM_DV), F32)] * N_PAIRS,
             [jnp.zeros((1, LANES), F32)] * N_PAIRS)
    chunk = (gq_ref[pl.ds(last * CHUNK, CHUNK), :], mqkv[last * CHUNK:], rg)
    (new_state,), _ = _chunk_steps([[chunk]], (0,), [state], False)
    _store_state(new_state, s0_ref.at[0], c0_ref.at[0], m0_ref.at[0])


def _scan_kernel(gqf_ref, mqf_ref, rgf_ref, gqb_ref, mqb_ref, rgb_ref, s0_ref, c0_ref, m0_ref,
                 of_ref, ob_ref, s_ref, c_ref, m_ref, *, cb):
    j = pl.program_id(1)

    @pl.when(j == 0)
    def _():
        s_ref[0] = s0_ref[0]
        c_ref[0] = c0_ref[0]
        m_ref[0] = m0_ref[0]
        s_ref[1] = jnp.zeros(s_ref.shape[1:], F32)
        c_ref[1] = jnp.zeros(c_ref.shape[1:], F32)
        m_ref[1] = jnp.zeros(m_ref.shape[1:], F32)

    chunk = lambda gq, mq, rg, c: (gq[0, c * CHUNK:(c + 1) * CHUNK, :], mq[0, c * CHUNK:(c + 1) * CHUNK, :], rg[0, c])
    order_b = list(range(cb - 1, -1, -1))
    in_f = [chunk(gqf_ref, mqf_ref, rgf_ref, c) for c in range(cb)]
    in_b = [chunk(gqb_ref, mqb_ref, rgb_ref, c) for c in order_b]
    st_f = (s_ref.at[0], c_ref.at[0], m_ref.at[0])
    st_b = (s_ref.at[1], c_ref.at[1], m_ref.at[1])
    (new_f, new_b), (out_f, out_b) = _chunk_steps([in_f, in_b], (0, 1), [_load_state(*st_f), _load_state(*st_b)], True)
    _store_state(new_f, *st_f)
    _store_state(new_b, *st_b)
    for c in range(cb):
        of_ref[0, c * CHUNK:(c + 1) * CHUNK, :] = jnp.concatenate(out_f[c], axis=1).astype(of_ref.dtype)
        cr = order_b[c]
        ob_ref[0, cr * CHUNK:(cr + 1) * CHUNK, :] = jnp.concatenate(out_b[c], axis=1).astype(ob_ref.dtype)


def _ffn_kernel(x_ref, of_ref, ob_ref, zo_ref, hn_ref, wout_ref, n1_ref, n2_ref, wg_ref, wu_ref, wd_ref, n3_ref,
                y_ref, *, ff_chunk):
    hn = hn_ref[...]
    sub = x_ref.shape[0] // FFN_SUBTILES
    tiles = [_Chain() for _ in range(FFN_SUBTILES)]
    ff_slices = [slice(lo, min(lo + ff_chunk, D_FF)) for lo in range(0, D_FF, ff_chunk)]

    def mix_stage(t, rows):
        o = of_ref[rows, :].astype(F32) + ob_ref[rows, :].astype(F32)
        zo = zo_ref[rows, :].astype(F32)
        parts = []
        for h in range(GDN_HEADS + MLSTM_HEADS):
            sl = slice(h * LANES, (h + 1) * LANES)
            parts.append((_rms(o[:, sl], hn[:, sl]) * zo[:, sl]).astype(BF16))
        t.mix = _dot(jnp.concatenate(parts, axis=1), wout_ref[...])

    def norm_stage(t, rows):
        t.x1 = x_ref[rows, :] + _rms(t.mix, n1_ref[...])
        t.h2 = _rms(t.x1, n2_ref[...]).astype(BF16)
        t.f = None

    def ff_stage(c):
        def run(t, rows):
            gate = _dot(t.h2, wg_ref[:, ff_slices[c]])
            up = _dot(t.h2, wu_ref[:, ff_slices[c]])
            act = (gate * jax.nn.sigmoid(gate) * up).astype(BF16)
            part = _dot(act, wd_ref[ff_slices[c], :])
            t.f = part if t.f is None else t.f + part
        return run

    def out_stage(t, rows):
        y_ref[rows, :] = t.x1 + _rms(t.f, n3_ref[...])

    stages = [mix_stage, norm_stage] + [ff_stage(c) for c in range(len(ff_slices))] + [out_stage]
    for step in range(len(stages) + FFN_SUBTILES - 1):
        for i, t in enumerate(tiles):
            if 0 <= step - i < len(stages):
                stages[step - i](t, slice(i * sub, (i + 1) * sub))


def _const_spec(shape):
    nd = len(shape)
    return pl.BlockSpec(shape, lambda *_: (0,) * nd, pipeline_mode=pl.Buffered(1))


def _encode(x, p):
    bsz, t, _ = x.shape
    tm = min(PREP_SUBTILES * PREP_SUB_ROWS, t)
    nt = t // tm
    cb = min(16, t // CHUNK)
    tb = cb * CHUNK
    nj = t // tb
    nc = t // CHUNK
    cparams = functools.partial(pltpu.CompilerParams, vmem_limit_bytes=VMEM_LIMIT)
    weight_specs = [
        _const_spec((1, D_MODEL)),
        _const_spec((D_MODEL, W1_COLS)),
        _const_spec((N_GATES, D_MODEL)),
        _const_spec((SUBLANES, QKV_W)),
        _const_spec((2 * SUBLANES, LANES)),
        _const_spec((2 * SUBLANES, LANES)),
    ]
    weights = (p["norm_pre_mix"], p["w1"], p["wg"], p["conv_w"], p["gadd"], p["galog"])
    s_shape = (N_PAIRS, GDN_DK, 2 * GDN_DV)
    c_shape = (N_PAIRS, MLSTM_DQK, 4 * MLSTM_DV)
    m_shape = (N_PAIRS, SUBLANES, LANES)
    state_specs = [pl.BlockSpec((1,) + s, lambda b, *_: (b, 0, 0, 0)) for s in (s_shape, c_shape, m_shape)]

    s0, c0, m0 = pl.pallas_call(
        _meta_kernel,
        grid=(bsz,),
        in_specs=[pl.BlockSpec((1, HALO, D_MODEL), lambda b: (b, 0, 0)),
                  _const_spec((N_META, D_MODEL))] + weight_specs,
        out_specs=state_specs,
        out_shape=[jax.ShapeDtypeStruct((bsz,) + s, F32) for s in (s_shape, c_shape, m_shape)],
        scratch_shapes=[
            pltpu.VMEM((META_ROWS, QKV_W), BF16),
        ],
        compiler_params=cparams(dimension_semantics=("arbitrary",)),
        name="meta_state",
    )(x, p["meta"], *weights)

    tpb = tm // HALO
    gqkv, zo, mqkv, rg = pl.pallas_call(
        functools.partial(_prep_kernel, tm=tm),
        grid=(bsz, nt),
        in_specs=[
            pl.BlockSpec((1, tm, D_MODEL), lambda b, i: (b, i, 0)),
            pl.BlockSpec((1, HALO, D_MODEL), lambda b, i: (b, jnp.maximum(i * tpb - 1, 0), 0)),
            pl.BlockSpec((1, HALO, D_MODEL), lambda b, i: (b, jnp.minimum((i + 1) * tpb, t // HALO - 1), 0)),
            pl.BlockSpec((HALO, D_MODEL), lambda b, i: (N_META // HALO - 1, 0)),
        ] + weight_specs,
        out_specs=[
            pl.BlockSpec((1, tm, QKV_W), lambda b, i: (b, i, 0)),
            pl.BlockSpec((1, tm, ZO_W), lambda b, i: (b, i, 0)),
            pl.BlockSpec((1, tm, MQKV_W), lambda b, i: (b, i, 0)),
            pl.BlockSpec((1, tm // CHUNK, RG_ROWS, LANES), lambda b, i: (b, i, 0, 0)),
        ],
        out_shape=[
            jax.ShapeDtypeStruct((bsz, t, QKV_W), BF16),
            jax.ShapeDtypeStruct((bsz, t, ZO_W), BF16),
            jax.ShapeDtypeStruct((bsz, t, MQKV_W), BF16),
            jax.ShapeDtypeStruct((bsz, nc, RG_ROWS, LANES), F32),
        ],
        compiler_params=cparams(dimension_semantics=("arbitrary", "arbitrary")),
        name="token_prep",
    )(x, x, x, p["meta"], *weights)

    fwd = lambda b, j: (b, j, 0)
    bwd = lambda b, j: (b, nj - 1 - j, 0)
    fwd4 = lambda b, j: (b, j, 0, 0)
    bwd4 = lambda b, j: (b, nj - 1 - j, 0, 0)
    o_f, o_b = pl.pallas_call(
        functools.partial(_scan_kernel, cb=cb),
        grid=(bsz, nj),
        in_specs=[
            pl.BlockSpec((1, tb, QKV_W), fwd),
            pl.BlockSpec((1, tb, MQKV_W), fwd),
            pl.BlockSpec((1, cb, RG_ROWS, LANES), fwd4),
            pl.BlockSpec((1, tb, QKV_W), bwd),
            pl.BlockSpec((1, tb, MQKV_W), bwd),
            pl.BlockSpec((1, cb, RG_ROWS, LANES), bwd4),
        ] + state_specs,
        out_specs=[
            pl.BlockSpec((1, tb, ZO_W), fwd),
            pl.BlockSpec((1, tb, ZO_W), bwd),
        ],
        out_shape=[
            jax.ShapeDtypeStruct((bsz, t, ZO_W), BF16),
            jax.ShapeDtypeStruct((bsz, t, ZO_W), BF16),
        ],
        scratch_shapes=[
            pltpu.VMEM((2,) + s_shape, F32),
            pltpu.VMEM((2,) + c_shape, F32),
            pltpu.VMEM((2,) + m_shape, F32),
        ],
        compiler_params=cparams(dimension_semantics=("arbitrary", "arbitrary")),
        name="chunk_scan",
    )(gqkv, mqkv, rg, gqkv, mqkv, rg, s0, c0, m0)

    rows = bsz * t
    tr = min(1024, rows)
    row_spec = lambda w: pl.BlockSpec((tr, w), lambda i: (i, 0))
    y = pl.pallas_call(
        functools.partial(_ffn_kernel, ff_chunk=FF_CHUNK),
        grid=(rows // tr,),
        in_specs=[
            row_spec(D_MODEL), row_spec(ZO_W), row_spec(ZO_W), row_spec(ZO_W),
            _const_spec((1, ZO_W)),
            _const_spec((ZO_W, D_MODEL)),
            _const_spec((1, D_MODEL)),
            _const_spec((1, D_MODEL)),
            _const_spec((D_MODEL, D_FF)),
            _const_spec((D_MODEL, D_FF)),
            _const_spec((D_FF, D_MODEL)),
            _const_spec((1, D_MODEL)),
        ],
        out_specs=row_spec(D_MODEL),
        out_shape=jax.ShapeDtypeStruct((rows, D_MODEL), F32),
        compiler_params=cparams(dimension_semantics=("arbitrary",)),
        name="mix_ffn",
    )(x.reshape(rows, D_MODEL), o_f.reshape(rows, ZO_W), o_b.reshape(rows, ZO_W), zo.reshape(rows, ZO_W),
      p["head_norm"], p["w_out"], p["norm_post_mix"], p["norm_pre_ffn"],
      p["w_gate"], p["w_up"], p["w_down"], p["norm_post_ffn"])
    return y.reshape(bsz, t, D_MODEL)


def _pair_rows(v):
    return jnp.repeat(v.reshape(2 * N_PAIRS, 2), CHUNK, axis=1)


def _prepare_params(meta_tokens, norm_pre_mix, w_in, conv_w, A_log, dt_bias, gdn_norm, i_bias, f_bias,
                    mlstm_norm, w_out, norm_post_mix, norm_pre_ffn, w_gate, w_up, w_down, norm_post_ffn):
    w = w_in[0]
    o_qkv = 0
    o_z = o_qkv + QKV_W
    o_a = o_z + GDN_HEADS * GDN_DV
    o_b = o_a + 2 * GDN_HEADS
    o_mq = o_b + 2 * GDN_HEADS
    o_o = o_mq + MQKV_W
    o_i = o_o + MLSTM_HEADS * MLSTM_DV
    o_f = o_i + 2 * MLSTM_HEADS
    w1 = jnp.concatenate([w[:, o_qkv:o_z], w[:, o_z:o_a], w[:, o_o:o_i], w[:, o_mq:o_o]], axis=1).astype(BF16)
    gcols = jnp.stack([w[:, o:o + 2 * GDN_HEADS] for o in (o_a, o_b, o_i, o_f)], axis=1)
    gcols = gcols.reshape(D_MODEL, 4, 2, N_PAIRS, 2)
    wg = jnp.transpose(gcols, (4, 1, 2, 3, 0)).reshape(N_GATES, D_MODEL).astype(BF16)
    zeros4 = jnp.zeros((2 * N_PAIRS, LANES), F32)
    gadd = jnp.concatenate([_pair_rows(dt_bias[0]), zeros4, _pair_rows(i_bias[0]), _pair_rows(f_bias[0])], axis=0)
    galog = jnp.concatenate([_pair_rows(A_log[0]), zeros4, zeros4, zeros4], axis=0)
    cw = jnp.zeros((SUBLANES, QKV_W), F32).at[0:CONV_K].set(conv_w[0])
    head_norm = jnp.concatenate([jnp.tile(gdn_norm[0], GDN_HEADS), mlstm_norm[0]]).reshape(1, ZO_W)
    return {
        "meta": meta_tokens,
        "norm_pre_mix": norm_pre_mix[0].reshape(1, D_MODEL),
        "w1": w1,
        "wg": wg,
        "conv_w": cw,
        "gadd": gadd,
        "galog": galog,
        "head_norm": head_norm,
        "w_out": w_out[0].astype(BF16),
        "norm_post_mix": norm_post_mix[0].reshape(1, D_MODEL),
        "norm_pre_ffn": norm_pre_ffn[0].reshape(1, D_MODEL),
        "w_gate": w_gate[0].astype(BF16),
        "w_up": w_up[0].astype(BF16),
        "w_down": w_down[0].astype(BF16),
        "norm_post_ffn": norm_post_ffn[0].reshape(1, D_MODEL),
    }


def kernel(x_prompt, x_sample, meta_tokens, norm_pre_mix, w_in, conv_w, A_log, dt_bias, gdn_norm, i_bias, f_bias,
           mlstm_norm, w_out, norm_post_mix, norm_pre_ffn, w_gate, w_up, w_down, norm_post_ffn):
    p = _prepare_params(meta_tokens, norm_pre_mix, w_in, conv_w, A_log, dt_bias, gdn_norm, i_bias, f_bias,
                        mlstm_norm, w_out, norm_post_mix, norm_pre_ffn, w_gate, w_up, w_down, norm_post_ffn)
    return (_encode(x_prompt, p), _encode(x_sample, p))
```
